```python
import jax, jax.numpy as jnp
from jax import lax
import numpy as np

D_MODEL = 1024
BATCH = 8
SEQ = 8192
DEPTH = 2

CHUNK = 64
D_MIX = D_MODEL
RWKV_WIDTH = D_MIX // 2
CONV_WIDTH = D_MIX - RWKV_WIDTH
HEAD_SIZE = 64
N_RWKV_HEADS = RWKV_WIDTH // HEAD_SIZE
CONV_K = 3
D_DECAY_LORA = 64
D_AAA_LORA = 64
D_MV_LORA = 32
RWKV_COLS = 4 * RWKV_WIDTH + D_DECAY_LORA + D_AAA_LORA
CONV_COLS = 4 * CONV_WIDTH
IN_COLS = RWKV_COLS + CONV_COLS
NORM_EPS = 1e-6
GN_EPS = 1e-5 * HEAD_SIZE
L2_EPS = 1e-12

kernel_name = "hybrid_rwkv7_shortconv_parallel_heads"


def rmsnorm(x, gain):
    xf = x.astype(jnp.float32)
    y = xf * lax.rsqrt(jnp.mean(xf * xf, axis=-1, keepdims=True) + NORM_EPS)
    return (y * gain.astype(jnp.float32)).astype(x.dtype)


def token_shift(z, mu):
    prev = jnp.pad(z, ((0, 0), (1, 0), (0, 0)))[:, :-1]
    return z + (prev - z) * mu


def wkv7_scan(r, decay, k, v, kk, b):
    bsz, _, n_heads, hs = r.shape

    def step(S, inp):
        r_t, w_t, k_t, v_t, kk_t, b_t = inp
        sa = -jnp.einsum("bhvk,bhk->bhv", S, kk_t)
        S = S * w_t[:, :, None, :] + sa[..., None] * b_t[:, :, None, :] + v_t[..., None] * k_t[:, :, None, :]
        y = jnp.einsum("bhvk,bhk->bhv", S, r_t)
        return S, y

    xs = tuple(jnp.moveaxis(t, 1, 0) for t in (r, decay, k, v, kk, b))
    S0 = jnp.zeros((bsz, n_heads, hs, hs), jnp.float32)
    _, ys = lax.scan(step, S0, xs)
    return jnp.moveaxis(ys, 0, 1)


def rwkv7_branch(z, mu, decay_up, decay_bias, aaa_up, aaa_bias, k_k, k_a, r_k, ln_gain, ln_bias,
                 v_first, v_down, v_up, v_bias):
    bsz, seq, _ = z.shape
    z = token_shift(z, mu)
    r, k, v, g, zw, za = jnp.split(
        z, [RWKV_WIDTH, 2 * RWKV_WIDTH, 3 * RWKV_WIDTH, 4 * RWKV_WIDTH, 4 * RWKV_WIDTH + D_DECAY_LORA], axis=-1)
    w_log = -jax.nn.softplus(-(decay_bias + jnp.tanh(zw) @ decay_up)) - 0.5
    decay = jnp.exp(-jnp.exp(w_log.astype(jnp.float32)))
    a = jax.nn.sigmoid(aaa_bias + za @ aaa_up)
    if v_first is None:
        v_first = v
    else:
        v = v + (v_first - v) * jax.nn.sigmoid(v_bias + (v @ v_down) @ v_up)

    heads = lambda t: t.astype(jnp.float32).reshape(bsz, seq, N_RWKV_HEADS, HEAD_SIZE)
    r_h, k_h, v_h, a_h, w_h = heads(r), heads(k), heads(v), heads(a), heads(decay)
    kk = k_h * heads(jnp.broadcast_to(k_k, k.shape))
    kk = kk / jnp.maximum(jnp.linalg.norm(kk, axis=-1, keepdims=True), L2_EPS)
    k_h = k_h * (1.0 + (a_h - 1.0) * heads(jnp.broadcast_to(k_a, k.shape)))

    y = wkv7_scan(r_h, w_h, k_h, v_h, kk, kk * a_h)
    mean = jnp.mean(y, axis=-1, keepdims=True)
    var = jnp.mean(jnp.square(y - mean), axis=-1, keepdims=True)
    y = (y - mean) * lax.rsqrt(var + GN_EPS)
    y = y.reshape(bsz, seq, RWKV_WIDTH) * ln_gain + ln_bias
    bonus = jnp.sum(r_h * k_h * r_k, axis=-1, keepdims=True) * v_h
    y = y + bonus.reshape(bsz, seq, RWKV_WIDTH)
    y = y.astype(z.dtype) * jax.nn.silu(g)
    return y, v_first


def short_conv_branch(z, conv_w):
    gate_b, gate_c, h, g = jnp.split(z, 4, axis=-1)
    u = gate_c * h
    u = lax.conv_general_dilated(
        u, conv_w[:, None, :].astype(u.dtype), window_strides=(1,), padding=[(CONV_K - 1, 0)],
        dimension_numbers=("NWC", "WIO", "NWC"), feature_group_count=CONV_WIDTH)
    return gate_b * u * jax.nn.silu(g)


def setup_inputs(seed: int = 0) -> dict:
    key = jax.random.key(seed)
    ks = jax.random.split(key, 20)
    nrm = lambda k, shape, s: jax.random.normal(k, shape, jnp.float32) * s
    n_v = DEPTH - 1
    return {
        "x": nrm(ks[0], (BATCH, SEQ, D_MODEL), 1.0),
        "norm_gain": 1.0 + nrm(ks[1], (DEPTH, D_MODEL), 0.1),
        "w_in": nrm(ks[2], (DEPTH, D_MODEL, IN_COLS), D_MODEL ** -0.5),
        "shift_mu": jax.random.uniform(ks[3], (DEPTH, RWKV_COLS), jnp.float32),
        "decay_up": nrm(ks[4], (DEPTH, D_DECAY_LORA, RWKV_WIDTH), 0.5 * D_DECAY_LORA ** -0.5),
        "decay_bias": jax.random.uniform(ks[5], (DEPTH, RWKV_WIDTH), jnp.float32, -5.0, 1.0),
        "aaa_up": nrm(ks[6], (DEPTH, D_AAA_LORA, RWKV_WIDTH), 0.5 * D_AAA_LORA ** -0.5),
        "aaa_bias": nrm(ks[7], (DEPTH, RWKV_WIDTH), 0.5),
        "k_k": 0.85 + nrm(ks[8], (DEPTH, RWKV_WIDTH), 0.1),
        "k_a": 1.0 + nrm(ks[9], (DEPTH, RWKV_WIDTH), 0.1),
        "r_k": nrm(ks[10], (DEPTH, N_RWKV_HEADS, HEAD_SIZE), 0.1),
        "ln_gain": 1.0 + nrm(ks[11], (DEPTH, RWKV_WIDTH), 0.1),
        "ln_bias": nrm(ks[12], (DEPTH, RWKV_WIDTH), 0.02),
        "v_down": nrm(ks[13], (n_v, RWKV_WIDTH, D_MV_LORA), RWKV_WIDTH ** -0.5),
        "v_up": nrm(ks[14], (n_v, D_MV_LORA, RWKV_WIDTH), 0.5 * D_MV_LORA ** -0.5),
        "v_bias": nrm(ks[15], (n_v, RWKV_WIDTH), 0.5),
        "conv_w": nrm(ks[16], (DEPTH, CONV_K, CONV_WIDTH), CONV_K ** -0.5),
        "w_out": nrm(ks[17], (DEPTH, D_MIX, D_MODEL), D_MIX ** -0.5),
        "final_gain": 1.0 + nrm(ks[18], (D_MODEL,), 0.1),
    }


def reference(x, norm_gain, w_in, shift_mu, decay_up, decay_bias, aaa_up, aaa_bias, k_k, k_a, r_k,
              ln_gain, ln_bias, v_down, v_up, v_bias, conv_w, w_out, final_gain):
    v_first = None
    for l in range(DEPTH):
        h = rmsnorm(x, norm_gain[l])
        z = h @ w_in[l]
        z_rwkv, z_conv = z[..., :RWKV_COLS], z[..., RWKV_COLS:]
        if l == 0:
            vd, vu, vb = None, None, None
        else:
            vd, vu, vb = v_down[l - 1], v_up[l - 1], v_bias[l - 1]
        y_rwkv, v_first = rwkv7_branch(
            z_rwkv, shift_mu[l], decay_up[l], decay_bias[l], aaa_up[l], aaa_bias[l], k_k[l], k_a[l], r_k[l],
            ln_gain[l], ln_bias[l], v_first, vd, vu, vb)
        y_conv = short_conv_branch(z_conv, conv_w[l])
        y = jnp.concatenate([y_rwkv, y_conv], axis=-1) @ w_out[l]
        x = x + y
    return rmsnorm(x, final_gain)
```

```python
import functools

import jax
import jax.numpy as jnp
from jax import lax
from jax.experimental import pallas as pl
from jax.experimental.pallas import tpu as pltpu

D_MODEL = 1024
RWKV_WIDTH = 512
CONV_WIDTH = 512
HEAD_SIZE = 64
CONV_K = 3
D_LORA = 64
RWKV_COLS = 4 * RWKV_WIDTH + 2 * D_LORA
CONV_COLS = 4 * CONV_WIDTH
NORM_EPS = 1e-6
GN_EPS = 1e-5 * HEAD_SIZE
L2_EPS = 1e-12

LANES = 128
CHUNK = 64
PAIR = 2 * HEAD_SIZE
N_PAIRS = RWKV_WIDTH // PAIR
PROJ_ROWS = 256
WKV_ROWS = 512
OUT_ROWS = 512
CARRY_ROWS = 8
VMEM_LIMIT = 56 * 1024 * 1024

F32 = jnp.float32
BF16 = jnp.bfloat16


def _dot(a, b):
    return jnp.dot(a.astype(BF16), b.astype(BF16), preferred_element_type=F32)


def _dot_split2(a, b):
    hi = a.astype(BF16)
    lo = (a - hi.astype(F32)).astype(BF16)
    return (jnp.dot(hi, b, preferred_element_type=F32)
            + jnp.dot(lo, b, preferred_element_type=F32))


def _shift_rows(x, carry, n):
    rows = lax.broadcasted_iota(jnp.int32, x.shape, 0)
    rolled = pltpu.roll(x, n, 0)
    out = rolled
    for j in range(n):
        out = jnp.where(rows == j, carry[CARRY_ROWS - n + j:CARRY_ROWS - n + j + 1, :], out)
    return out


def _proj_kernel(has_vres, *refs):
    (x_ref, gain_ref, win_ref, mu_ref, dup_ref, dbias_ref, aup_ref, abias_ref, kk_ref, ka_ref,
     rk_ref, convw_ref, hones_ref) = refs[:13]
    pos = 13
    if has_vres:
        vfirst_ref, vdown_ref, vup_ref, vbias_ref = refs[pos:pos + 4]
        pos += 4
    (r_out, lw_out, k_out, v_out, kk_out, b_out, bonus_out, sg_out, yconv_out) = refs[pos:pos + 9]
    zlast_ref, ulast_ref = refs[pos + 9:]

    @pl.when(pl.program_id(1) == 0)
    def _():
        zlast_ref[...] = jnp.zeros_like(zlast_ref)
        ulast_ref[...] = jnp.zeros_like(ulast_ref)

    x = x_ref[...]
    h = x * lax.rsqrt(jnp.mean(x * x, axis=-1, keepdims=True) + NORM_EPS) * gain_ref[...]
    hb = h.astype(BF16)

    z = jnp.dot(hb, win_ref[:, :RWKV_COLS], preferred_element_type=F32)
    prev = _shift_rows(z, zlast_ref[...], 1)
    zlast_ref[...] = z[PROJ_ROWS - CARRY_ROWS:, :]
    z = z + (prev - z) * mu_ref[...]
    W = RWKV_WIDTH
    r, k, v, g, zwa = z[:, :W], z[:, W:2 * W], z[:, 2 * W:3 * W], z[:, 3 * W:4 * W], z[:, 4 * W:]
    lw = -jnp.exp(-0.5) * jax.nn.sigmoid(dbias_ref[...] + _dot(jnp.tanh(zwa), dup_ref[...]))
    a = jax.nn.sigmoid(abias_ref[...] + _dot(zwa, aup_ref[...]))
    if has_vres:
        mix = jax.nn.sigmoid(vbias_ref[...] + _dot(_dot(v, vdown_ref[...]), vup_ref[...]))
        v = v + (vfirst_ref[...] - v) * mix
    hones = hones_ref[...]
    kk = k * kk_ref[...]
    norm = jnp.sqrt(_dot_split2(kk * kk, hones))
    kk = kk / jnp.maximum(norm, L2_EPS)
    k = k * (1.0 + (a - 1.0) * ka_ref[...])
    bonus = _dot_split2(r * k * rk_ref[...], hones) * v
    r_out[...] = r
    lw_out[...] = lw
    k_out[...] = k
    v_out[...] = v
    kk_out[...] = kk
    b_out[...] = kk * a
    bonus_out[...] = bonus
    sg_out[...] = g * jax.nn.sigmoid(g)

    zc = jnp.dot(hb, win_ref[:, RWKV_COLS:], preferred_element_type=F32)
    C = CONV_WIDTH
    gate_b, gate_c, hc, gc = zc[:, :C], zc[:, C:2 * C], zc[:, 2 * C:3 * C], zc[:, 3 * C:]
    u = gate_c * hc
    ulast = ulast_ref[...]
    u1 = _shift_rows(u, ulast, 1)
    u2 = _shift_rows(u, ulast, 2)
    ulast_ref[...] = u[PROJ_ROWS - CARRY_ROWS:, :]
    cw = convw_ref[...]
    conv = u2 * cw[0:1, :] + u1 * cw[1:2, :] + u * cw[2:3, :]
    yconv_out[...] = (gate_b * conv * (gc * jax.nn.sigmoid(gc))).astype(yconv_out.dtype)


def _const_spec(shape):
    return pl.BlockSpec(shape, lambda b, t: (0,) * len(shape))


def _proj_call(x, gain, win, mu, dup, dbias, aup, abias, k_k, k_a, rk, convw, hones, vres):
    B, T, _ = x.shape
    has_vres = vres is not None
    row_spec = lambda width: pl.BlockSpec((None, PROJ_ROWS, width), lambda b, t: (b, t, 0))
    vec = lambda n: _const_spec((1, n))
    in_specs = [row_spec(D_MODEL), vec(D_MODEL), _const_spec(win.shape), vec(RWKV_COLS),
                _const_spec(dup.shape), vec(RWKV_WIDTH), _const_spec(aup.shape), vec(RWKV_WIDTH),
                vec(RWKV_WIDTH), vec(RWKV_WIDTH), vec(RWKV_WIDTH), _const_spec(convw.shape),
                _const_spec(hones.shape)]
    args = [x, gain, win, mu, dup, dbias, aup, abias, k_k, k_a, rk, convw, hones]
    if has_vres:
        vfirst, vdown, vup, vbias = vres
        in_specs += [row_spec(RWKV_WIDTH), _const_spec(vdown.shape), _const_spec(vup.shape),
                     vec(RWKV_WIDTH)]
        args += [vfirst, vdown, vup, vbias]
    f32_out = jax.ShapeDtypeStruct((B, T, RWKV_WIDTH), F32)
    out_shape = [f32_out] * 8 + [jax.ShapeDtypeStruct((B, T, CONV_WIDTH), BF16)]
    return pl.pallas_call(
        functools.partial(_proj_kernel, has_vres),
        grid=(B, T // PROJ_ROWS),
        in_specs=in_specs,
        out_specs=[row_spec(RWKV_WIDTH)] * 9,
        out_shape=out_shape,
        scratch_shapes=[pltpu.VMEM((CARRY_ROWS, RWKV_COLS), F32),
                        pltpu.VMEM((CARRY_ROWS, CONV_WIDTH), F32)],
        compiler_params=pltpu.CompilerParams(
            dimension_semantics=("parallel", "arbitrary"), vmem_limit_bytes=VMEM_LIMIT),
    )(*args)


def _wkv_kernel(r_ref, lw_ref, k_ref, v_ref, kk_ref, b_ref, bonus_ref, sg_ref, lng_ref, lnb_ref,
                tri_ref, gmean_ref, y_out, state_ref):
    @pl.when(pl.program_id(2) == 0)
    def _():
        state_ref[...] = jnp.zeros_like(state_ref)

    lane = lax.broadcasted_iota(jnp.int32, (CHUNK, PAIR), 1)
    row = lax.broadcasted_iota(jnp.int32, (CHUNK, PAIR), 0)
    head0 = lane < HEAD_SIZE
    col = jnp.where(head0, lane, lane - HEAD_SIZE)
    strict = col < row
    incl = col <= row
    eye = (col == row).astype(F32)
    lane2 = lax.broadcasted_iota(jnp.int32, (PAIR, PAIR), 1)
    row2 = lax.broadcasted_iota(jnp.int32, (PAIR, PAIR), 0)
    same_head = (lane2 < HEAD_SIZE) == (row2 < HEAD_SIZE)
    diag2 = lane2 == row2

    def bd(m):
        return jnp.concatenate([jnp.where(head0, m, 0.0), jnp.where(head0, 0.0, m)], axis=0)

    tri3 = tri_ref[...]
    gmean = gmean_ref[...]
    lng = lng_ref[...]
    lnb = lnb_ref[...]
    zeros_pp = jnp.zeros((PAIR, PAIR), F32)
    zeros_cp = jnp.zeros((CHUNK, PAIR), F32)

    state = state_ref[...]
    for c in range(WKV_ROWS // CHUNK):
        rows = pl.ds(c * CHUNK, CHUNK)
        r, lw, k, v, kk, b = (ref[rows, :] for ref in (r_ref, lw_ref, k_ref, v_ref, kk_ref, b_ref))

        l1 = lw.astype(BF16)
        rem = lw - l1.astype(F32)
        l2 = rem.astype(BF16)
        l3 = (rem - l2.astype(F32)).astype(BF16)
        cum = jnp.dot(tri3, jnp.concatenate([l1, l2, l3], axis=0), preferred_element_type=F32)
        cl = cum[CHUNK - 1:CHUNK, :]
        ra = r * jnp.exp(cum)
        aa = -kk * jnp.exp(cum - lw)
        e_neg = jnp.exp(-cum)
        bb = b * e_neg
        kb = k * e_neg
        e_end = jnp.exp(cl - cum)
        be = b * e_end
        ke = k * e_end
        w_end = jnp.exp(cl)

        gram = lax.dot_general(
            jnp.concatenate([aa, ra], axis=0).astype(BF16),
            jnp.concatenate([bd(bb), bd(kb)], axis=0).astype(BF16),
            (((1,), (1,)), ((), ())), preferred_element_type=F32)
        n_ab = jnp.where(strict, gram[:CHUNK, :PAIR], 0.0)
        a_ak = jnp.where(strict, gram[:CHUNK, PAIR:], 0.0)
        a_rb = jnp.where(incl, gram[CHUNK:, :PAIR], 0.0)
        a_rk = jnp.where(incl, gram[CHUNK:, PAIR:], 0.0)

        tinv = eye + n_ab
        power = n_ab
        for _ in range(5):
            power = _dot(power, bd(power))
            tinv = tinv + _dot(tinv, bd(power))

        akv = _dot(a_ak, bd(v))
        pq = _dot(tinv, jnp.concatenate([bd(aa), bd(akv)], axis=1))
        p_m, q_m = pq[:, :PAIR], pq[:, PAIR:]
        zmat = jnp.concatenate(
            [jnp.concatenate([bd(p_m), bd(q_m)], axis=1),
             jnp.concatenate([zeros_pp, bd(v)], axis=1)], axis=0)
        gh = _dot(jnp.concatenate([a_rb, a_rk], axis=1), zmat)
        g_m = gh[:, :PAIR] + ra
        h_m = gh[:, PAIR:]

        end_t = jnp.concatenate([be, ke], axis=0).T
        rhs = jnp.concatenate(
            [jnp.concatenate([p_m, q_m], axis=1),
             jnp.concatenate([zeros_cp, v], axis=1)], axis=0)
        mn = _dot(end_t, rhs)
        m_bd = jnp.where(same_head, mn[:, :PAIR], 0.0) + jnp.where(diag2, w_end, 0.0)
        n_bd = jnp.where(same_head, mn[:, PAIR:], 0.0)

        state_b = state.astype(BF16)
        y = jnp.dot(g_m.astype(BF16), state_b, preferred_element_type=F32) + h_m
        state = jnp.dot(m_bd.astype(BF16), state_b, preferred_element_type=F32) + n_bd

        d = y - _dot_split2(y, gmean)
        var = _dot_split2(d * d, gmean)
        yn = d * lax.rsqrt(var + GN_EPS) * lng + lnb
        y_out[rows, :] = ((yn + bonus_ref[rows, :]) * sg_ref[rows, :]).astype(y_out.dtype)
    state_ref[...] = state


def _wkv_call(r, lw, k, v, kk, b, bonus, sg, lng, lnb, tri3, gmean):
    B, T, _ = r.shape
    blk = pl.BlockSpec((None, WKV_ROWS, PAIR), lambda bi, p, t: (bi, t, p))
    vec = pl.BlockSpec((1, PAIR), lambda bi, p, t: (0, p))
    const = lambda a: pl.BlockSpec(a.shape, lambda bi, p, t: (0, 0))
    return pl.pallas_call(
        _wkv_kernel,
        grid=(B, N_PAIRS, T // WKV_ROWS),
        in_specs=[blk] * 8 + [vec, vec, const(tri3), const(gmean)],
        out_specs=blk,
        out_shape=jax.ShapeDtypeStruct((B, T, RWKV_WIDTH), BF16),
        scratch_shapes=[pltpu.VMEM((PAIR, PAIR), F32)],
        compiler_params=pltpu.CompilerParams(
            dimension_semantics=("parallel", "parallel", "arbitrary"),
            vmem_limit_bytes=VMEM_LIMIT),
    )(r, lw, k, v, kk, b, bonus, sg, lng, lnb, tri3, gmean)


def _out_kernel(final, *refs):
    if final:
        x_ref, yr_ref, yc_ref, wout_ref, fgain_ref, o_ref = refs
    else:
        x_ref, yr_ref, yc_ref, wout_ref, o_ref = refs
    y = (jnp.dot(yr_ref[...], wout_ref[:RWKV_WIDTH, :], preferred_element_type=F32)
         + jnp.dot(yc_ref[...], wout_ref[RWKV_WIDTH:, :], preferred_element_type=F32))
    x = x_ref[...] + y
    if final:
        x = x * lax.rsqrt(jnp.mean(x * x, axis=-1, keepdims=True) + NORM_EPS) * fgain_ref[...]
    o_ref[...] = x


def _out_call(x2, yr2, yc2, wout, fgain):
    n = x2.shape[0]
    final = fgain is not None
    rows = lambda width: pl.BlockSpec((OUT_ROWS, width), lambda i: (i, 0))
    in_specs = [rows(D_MODEL), rows(RWKV_WIDTH), rows(CONV_WIDTH),
                pl.BlockSpec(wout.shape, lambda i: (0, 0))]
    args = [x2, yr2, yc2, wout]
    if final:
        in_specs.append(pl.BlockSpec((1, D_MODEL), lambda i: (0, 0)))
        args.append(fgain)
    return pl.pallas_call(
        functools.partial(_out_kernel, final),
        grid=(n // OUT_ROWS,),
        in_specs=in_specs,
        out_specs=rows(D_MODEL),
        out_shape=jax.ShapeDtypeStruct((n, D_MODEL), F32),
        compiler_params=pltpu.CompilerParams(
            dimension_semantics=("parallel",), vmem_limit_bytes=VMEM_LIMIT),
    )(*args)


def kernel(x, norm_gain, w_in, shift_mu, decay_up, decay_bias, aaa_up, aaa_bias, k_k, k_a, r_k,
           ln_gain, ln_bias, v_down, v_up, v_bias, conv_w, w_out, final_gain):
    depth = w_in.shape[0]
    B, T, _ = x.shape
    assert T % WKV_ROWS == 0 and T % PROJ_ROWS == 0 and (B * T) % OUT_ROWS == 0

    idx = jnp.arange(RWKV_WIDTH)
    hones = (idx[:, None] // HEAD_SIZE == idx[None, :] // HEAD_SIZE).astype(BF16)
    pidx = jnp.arange(PAIR)
    gmean = ((pidx[:, None] // HEAD_SIZE == pidx[None, :] // HEAD_SIZE).astype(F32)
             / HEAD_SIZE).astype(BF16)
    tri = jnp.tril(jnp.ones((CHUNK, CHUNK), BF16))
    tri3 = jnp.concatenate([tri, tri, tri], axis=1)
    lora_pad = jnp.zeros((D_LORA, RWKV_WIDTH), F32)

    row = lambda a: a.reshape(1, -1)
    v_first = None
    for l in range(depth):
        dup = jnp.concatenate([decay_up[l], lora_pad], axis=0).astype(BF16)
        aup = jnp.concatenate([lora_pad, aaa_up[l]], axis=0).astype(BF16)
        vres = None
        if l > 0:
            vres = (v_first, v_down[l - 1].astype(BF16), v_up[l - 1].astype(BF16), row(v_bias[l - 1]))
        r, lw, k, v, kk, b, bonus, sg, yconv = _proj_call(
            x, row(norm_gain[l]), w_in[l].astype(BF16), row(shift_mu[l]), dup, row(decay_bias[l]),
            aup, row(aaa_bias[l]), row(k_k[l]), row(k_a[l]), row(r_k[l]), conv_w[l], hones, vres)
        if l == 0:
            v_first = v
        y_rwkv = _wkv_call(r, lw, k, v, kk, b, bonus, sg, row(ln_gain[l]), row(ln_bias[l]),
                           tri3, gmean)
        fgain = row(final_gain) if l == depth - 1 else None
        x = _out_call(x.reshape(B * T, D_MODEL), y_rwkv.reshape(B * T, RWKV_WIDTH),
                      yconv.reshape(B * T, CONV_WIDTH), w_out[l].astype(BF16),
                      fgain).reshape(B, T, D_MODEL)
    return x
```

```python
import functools

import jax
import jax.numpy as jnp
from jax import lax
from jax.experimental import pallas as pl
from jax.experimental.pallas import tpu as pltpu

D_MODEL = 1024
RWKV_WIDTH = 512
CONV_WIDTH = 512
HEAD_SIZE = 64
CONV_K = 3
D_LORA = 64
RWKV_COLS = 4 * RWKV_WIDTH + 2 * D_LORA
CONV_COLS = 4 * CONV_WIDTH
NORM_EPS = 1e-6
GN_EPS = 1e-5 * HEAD_SIZE
L2_EPS = 1e-12

MXU_DIM = 256
CHUNK = 64
PAIR = 2 * HEAD_SIZE
N_PAIRS = RWKV_WIDTH // PAIR
PROJ_ROWS = 256
WKV_ROWS = 256
OUT_ROWS = 512
CARRY_ROWS = 8
VMEM_LIMIT = 56 * 1024 * 1024

F32 = jnp.float32
BF16 = jnp.bfloat16


def _dot(a, b):
    return jnp.dot(a.astype(BF16), b.astype(BF16), preferred_element_type=F32)


def _head_sums(x, hones):
    xb = x.astype(BF16)
    return jnp.concatenate(
        [jnp.dot(xb[:, i:i + MXU_DIM], hones, preferred_element_type=F32)
         for i in range(0, x.shape[1], MXU_DIM)], axis=1)


def _shift_rows(x, carry, n):
    rows = lax.broadcasted_iota(jnp.int32, x.shape, 0)
    rolled = pltpu.roll(x, n, 0)
    out = rolled
    for j in range(n):
        out = jnp.where(rows == j, carry[CARRY_ROWS - n + j:CARRY_ROWS - n + j + 1, :], out)
    return out


def _proj_kernel(has_vres, *refs):
    (x_ref, gain_ref, win_ref, mu_ref, dup_ref, dbias_ref, aup_ref, abias_ref, kk_ref, ka_ref,
     rk_ref, convw_ref, hones_ref) = refs[:13]
    pos = 13
    if has_vres:
        vfirst_ref, vdown_ref, vup_ref, vbias_ref = refs[pos:pos + 4]
        pos += 4
    (r_out, lw_out, k_out, v_out, kk_out, b_out, bonus_out, sg_out, yconv_out) = refs[pos:pos + 9]
    zlast_ref, ulast_ref = refs[pos + 9:]

    @pl.when(pl.program_id(1) == 0)
    def _():
        zlast_ref[...] = jnp.zeros_like(zlast_ref)
        ulast_ref[...] = jnp.zeros_like(ulast_ref)

    x = x_ref[...]
    h = x * lax.rsqrt(jnp.mean(x * x, axis=-1, keepdims=True) + NORM_EPS) * gain_ref[...]
    hb = h.astype(BF16)

    z = jnp.dot(hb, win_ref[:, :RWKV_COLS], preferred_element_type=F32)
    prev = _shift_rows(z, zlast_ref[...], 1)
    zlast_ref[...] = z[PROJ_ROWS - CARRY_ROWS:, :]
    z = z + (prev - z) * mu_ref[...]
    W = RWKV_WIDTH
    r, k, v, g, zwa = z[:, :W], z[:, W:2 * W], z[:, 2 * W:3 * W], z[:, 3 * W:4 * W], z[:, 4 * W:]
    lw = -jnp.exp(-0.5) * jax.nn.sigmoid(dbias_ref[...] + _dot(jnp.tanh(zwa), dup_ref[...]))
    a = jax.nn.sigmoid(abias_ref[...] + _dot(zwa, aup_ref[...]))
    if has_vres:
        mix = jax.nn.sigmoid(vbias_ref[...] + _dot(_dot(v, vdown_ref[...]), vup_ref[...]))
        v = v + (vfirst_ref[...] - v) * mix
    hones = hones_ref[...]
    kk = k * kk_ref[...]
    norm = jnp.sqrt(_head_sums(kk * kk, hones))
    kk = kk / jnp.maximum(norm, L2_EPS)
    k = k * (1.0 + (a - 1.0) * ka_ref[...])
    bonus = _head_sums(r * k * rk_ref[...], hones) * v
    r_out[...] = r
    lw_out[...] = lw
    k_out[...] = k
    v_out[...] = v
    kk_out[...] = kk
    b_out[...] = kk * a
    bonus_out[...] = bonus
    sg_out[...] = g * jax.nn.sigmoid(g)

    zc = jnp.dot(hb, win_ref[:, RWKV_COLS:], preferred_element_type=F32)
    C = CONV_WIDTH
    gate_b, gate_c, hc, gc = zc[:, :C], zc[:, C:2 * C], zc[:, 2 * C:3 * C], zc[:, 3 * C:]
    u = gate_c * hc
    ulast = ulast_ref[...]
    u1 = _shift_rows(u, ulast, 1)
    u2 = _shift_rows(u, ulast, 2)
    ulast_ref[...] = u[PROJ_ROWS - CARRY_ROWS:, :]
    cw = convw_ref[...]
    conv = u2 * cw[0:1, :] + u1 * cw[1:2, :] + u * cw[2:3, :]
    yconv_out[...] = (gate_b * conv * (gc * jax.nn.sigmoid(gc))).astype(yconv_out.dtype)


def _const_spec(shape):
    return pl.BlockSpec(shape, lambda b, t: (0,) * len(shape))


def _proj_call(x, gain, win, mu, dup, dbias, aup, abias, k_k, k_a, rk, convw, hones, vres):
    B, T, _ = x.shape
    has_vres = vres is not None
    row_spec = lambda width: pl.BlockSpec((None, PROJ_ROWS, width), lambda b, t: (b, t, 0))
    vec = lambda n: _const_spec((1, n))
    in_specs = [row_spec(D_MODEL), vec(D_MODEL), _const_spec(win.shape), vec(RWKV_COLS),
                _const_spec(dup.shape), vec(RWKV_WIDTH), _const_spec(aup.shape), vec(RWKV_WIDTH),
                vec(RWKV_WIDTH), vec(RWKV_WIDTH), vec(RWKV_WIDTH), _const_spec(convw.shape),
                _const_spec(hones.shape)]
    args = [x, gain, win, mu, dup, dbias, aup, abias, k_k, k_a, rk, convw, hones]
    if has_vres:
        vfirst, vdown, vup, vbias = vres
        in_specs += [row_spec(RWKV_WIDTH), _const_spec(vdown.shape), _const_spec(vup.shape),
                     vec(RWKV_WIDTH)]
        args += [vfirst, vdown, vup, vbias]
    f32_out = jax.ShapeDtypeStruct((B, T, RWKV_WIDTH), F32)
    out_shape = [f32_out] * 8 + [jax.ShapeDtypeStruct((B, T, CONV_WIDTH), BF16)]
    return pl.pallas_call(
        functools.partial(_proj_kernel, has_vres),
        grid=(B, T // PROJ_ROWS),
        in_specs=in_specs,
        out_specs=[row_spec(RWKV_WIDTH)] * 9,
        out_shape=out_shape,
        scratch_shapes=[pltpu.VMEM((CARRY_ROWS, RWKV_COLS), F32),
                        pltpu.VMEM((CARRY_ROWS, CONV_WIDTH), F32)],
        compiler_params=pltpu.CompilerParams(
            dimension_semantics=("parallel", "arbitrary"), vmem_limit_bytes=VMEM_LIMIT),
    )(*args)


def _wkv_kernel(r_ref, lw_ref, k_ref, v_ref, kk_ref, b_ref, bonus_ref, sg_ref, lng_ref, lnb_ref,
                tri_ref, gmean_ref, y_out, state_ref):
    @pl.when(pl.program_id(1) == 0)
    def _():
        state_ref[...] = jnp.zeros_like(state_ref)

    lane = lax.broadcasted_iota(jnp.int32, (CHUNK, PAIR), 1)
    row = lax.broadcasted_iota(jnp.int32, (CHUNK, PAIR), 0)
    head0 = lane < HEAD_SIZE
    col = jnp.where(head0, lane, lane - HEAD_SIZE)
    strict = col < row
    incl = col <= row
    eye_mask = col == row
    eye = eye_mask.astype(F32)

    def bd(m):
        return jnp.concatenate([jnp.where(head0, m, 0.0), jnp.where(head0, 0.0, m)], axis=0)

    tri3 = tri_ref[...]
    gmean = gmean_ref[...]
    zeros_pp = jnp.zeros((PAIR, PAIR), F32)

    n_chunks = WKV_ROWS // CHUNK
    each = lambda f, *cols: [f(*xs) for xs in zip(*cols)]
    tiles = [(pl.ds(c * CHUNK, CHUNK), pl.ds(p * PAIR, PAIR))
             for c in range(n_chunks) for p in range(N_PAIRS)]
    load = lambda ref: [ref[rw, ln] for rw, ln in tiles]
    r, lw, k, v, kk, b = (load(ref) for ref in (r_ref, lw_ref, k_ref, v_ref, kk_ref, b_ref))

    def running_sum(x):
        l1 = x.astype(BF16)
        rem = x - l1.astype(F32)
        l2 = rem.astype(BF16)
        l3 = (rem - l2.astype(F32)).astype(BF16)
        return jnp.dot(tri3, jnp.concatenate([l1, l2, l3], axis=0), preferred_element_type=F32)

    cum = each(running_sum, lw)
    cl = [x[CHUNK - 1:CHUNK, :] for x in cum]
    ra = each(lambda x, c: x * jnp.exp(c), r, cum)
    aa = each(lambda x, c, l: -x * jnp.exp(c - l), kk, cum, lw)
    e_neg = each(lambda c: jnp.exp(-c), cum)
    bb = each(jnp.multiply, b, e_neg)
    kb = each(jnp.multiply, k, e_neg)
    e_end = each(lambda e, c: jnp.exp(e - c), cl, cum)
    be = each(jnp.multiply, b, e_end)
    ke = each(jnp.multiply, k, e_end)
    w_end = each(jnp.exp, cl)

    gram = each(lambda a_, r_, b_, k_: lax.dot_general(
        jnp.concatenate([a_, r_], axis=0).astype(BF16),
        jnp.concatenate([bd(b_), bd(k_)], axis=0).astype(BF16),
        (((1,), (1,)), ((), ())), preferred_element_type=F32), aa, ra, bb, kb)
    n_ab = [jnp.where(strict, g[:CHUNK, :PAIR], 0.0) for g in gram]
    a_ak = [jnp.where(strict, g[:CHUNK, PAIR:], 0.0) for g in gram]
    a_rb = [jnp.where(incl, g[CHUNK:, :PAIR], 0.0) for g in gram]
    a_rk = [jnp.where(incl, g[CHUNK:, PAIR:], 0.0) for g in gram]

    tinv = [eye + n for n in n_ab]
    power = each(lambda n: _dot(n, bd(n)), n_ab)
    for _ in range(4):
        both = each(lambda p, t: _dot(jnp.concatenate([p, t], axis=0), bd(p)), power, tinv)
        power = [x[:CHUNK] for x in both]
        tinv = each(lambda t, x: t + x[CHUNK:], tinv, both)
    tinv = each(lambda t, p: t + _dot(t, bd(p)), tinv, power)

    akv = each(lambda a_, v_: _dot(a_, bd(v_)), a_ak, v)
    pq = each(lambda t, a_, q: _dot(t, jnp.concatenate([bd(a_), bd(q)], axis=1)), tinv, aa, akv)

    def fold_t(m):
        t = bd(m).T
        return t[:CHUNK] + t[CHUNK:]

    ghmn = each(lambda rb, rk, b_, k_, x, v_: _dot(
        jnp.concatenate([jnp.concatenate([rb, rk], axis=1),
                         jnp.concatenate([fold_t(b_), fold_t(k_)], axis=1)], axis=0),
        jnp.concatenate([jnp.concatenate([bd(x[:, :PAIR]), bd(x[:, PAIR:])], axis=1),
                         jnp.concatenate([zeros_pp, bd(v_)], axis=1)], axis=0)),
        a_rb, a_rk, be, ke, pq, v)
    gm_lhs = each(lambda x, r_, w: jnp.concatenate(
        [x[:CHUNK, :PAIR] + r_, x[CHUNK:, :PAIR] + jnp.where(eye_mask, w, 0.0)], axis=0).astype(BF16),
        ghmn, ra, w_end)

    state = [state_ref[p] for p in range(N_PAIRS)]
    y = []
    for c in range(n_chunks):
        for p in range(N_PAIRS):
            i = c * N_PAIRS + p
            both = jnp.dot(gm_lhs[i], bd(state[p]).astype(BF16), preferred_element_type=F32)
            y.append(both[:CHUNK] + ghmn[i][:CHUNK, PAIR:])
            state[p] = both[CHUNK:] + ghmn[i][CHUNK:, PAIR:]
    for p in range(N_PAIRS):
        state_ref[p] = state[p]

    y = jnp.concatenate(y, axis=0)
    y_hi = y.astype(BF16)
    y_lo = (y - y_hi.astype(F32)).astype(BF16)
    d = y - jnp.dot(jnp.concatenate([y_hi, y_lo], axis=1), gmean, preferred_element_type=F32)
    var = jnp.dot((d * d).astype(BF16), gmean[:PAIR], preferred_element_type=F32)
    yn = d * lax.rsqrt(var + GN_EPS)
    for i, (rw, ln) in enumerate(tiles):
        yi = yn[i * CHUNK:(i + 1) * CHUNK] * lng_ref[:, ln] + lnb_ref[:, ln]
        y_out[rw, ln] = ((yi + bonus_ref[rw, ln]) * sg_ref[rw, ln]).astype(y_out.dtype)


def _wkv_call(r, lw, k, v, kk, b, bonus, sg, lng, lnb, tri3, gmean):
    B, T, _ = r.shape
    blk = pl.BlockSpec((None, WKV_ROWS, RWKV_WIDTH), lambda bi, t: (bi, t, 0))
    const = lambda a: pl.BlockSpec(a.shape, lambda bi, t: (0, 0))
    return pl.pallas_call(
        _wkv_kernel,
        grid=(B, T // WKV_ROWS),
        in_specs=[blk] * 8 + [const(lng), const(lnb), const(tri3), const(gmean)],
        out_specs=blk,
        out_shape=jax.ShapeDtypeStruct((B, T, RWKV_WIDTH), BF16),
        scratch_shapes=[pltpu.VMEM((N_PAIRS, CHUNK, PAIR), F32)],
        compiler_params=pltpu.CompilerParams(
            dimension_semantics=("parallel", "arbitrary"), vmem_limit_bytes=VMEM_LIMIT),
    )(r, lw, k, v, kk, b, bonus, sg, lng, lnb, tri3, gmean)


def _out_kernel(final, *refs):
    if final:
        x_ref, yr_ref, yc_ref, wout_ref, fgain_ref, o_ref = refs
    else:
        x_ref, yr_ref, yc_ref, wout_ref, o_ref = refs
    y = (jnp.dot(yr_ref[...], wout_ref[:RWKV_WIDTH, :], preferred_element_type=F32)
         + jnp.dot(yc_ref[...], wout_ref[RWKV_WIDTH:, :], preferred_element_type=F32))
    x = x_ref[...] + y
    if final:
        x = x * lax.rsqrt(jnp.mean(x * x, axis=-1, keepdims=True) + NORM_EPS) * fgain_ref[...]
    o_ref[...] = x


def _out_call(x2, yr2, yc2, wout, fgain):
    n = x2.shape[0]
    final = fgain is not None
    rows = lambda width: pl.BlockSpec((OUT_ROWS, width), lambda i: (i, 0))
    in_specs = [rows(D_MODEL), rows(RWKV_WIDTH), rows(CONV_WIDTH),
                pl.BlockSpec(wout.shape, lambda i: (0, 0))]
    args = [x2, yr2, yc2, wout]
    if final:
        in_specs.append(pl.BlockSpec((1, D_MODEL), lambda i: (0, 0)))
        args.append(fgain)
    return pl.pallas_call(
        functools.partial(_out_kernel, final),
        grid=(n // OUT_ROWS,),
        in_specs=in_specs,
        out_specs=rows(D_MODEL),
        out_shape=jax.ShapeDtypeStruct((n, D_MODEL), F32),
        compiler_params=pltpu.CompilerParams(
            dimension_semantics=("parallel",), vmem_limit_bytes=VMEM_LIMIT),
    )(*args)


def kernel(x, norm_gain, w_in, shift_mu, decay_up, decay_bias, aaa_up, aaa_bias, k_k, k_a, r_k,
           ln_gain, ln_bias, v_down, v_up, v_bias, conv_w, w_out, final_gain):
    depth = w_in.shape[0]
    B, T, _ = x.shape
    assert T % WKV_ROWS == 0 and T % PROJ_ROWS == 0 and (B * T) % OUT_ROWS == 0

    idx = jnp.arange(MXU_DIM)
    hones =(idx[:, None] // HEAD_SIZE == idx[None, :] // HEAD_SIZE).astype(BF16)
    pidx = jnp.arange(PAIR)
    gmean = ((pidx[:, None] // HEAD_SIZE == pidx[None, :] // HEAD_SIZE).astype(F32)
             / HEAD_SIZE).astype(BF16)
    gmean = jnp.concatenate([gmean, gmean], axis=0)
    tri = jnp.tril(jnp.ones((CHUNK, CHUNK), BF16))
    tri3 = jnp.concatenate([tri, tri, tri], axis=1)
    lora_pad = jnp.zeros((D_LORA, RWKV_WIDTH), F32)

    row = lambda a: a.reshape(1, -1)
    v_first = None
    for l in range(depth):
        dup = jnp.concatenate([decay_up[l], lora_pad], axis=0).astype(BF16)
        aup = jnp.concatenate([lora_pad, aaa_up[l]], axis=0).astype(BF16)
        vres = None
        if l > 0:
            vres = (v_first, v_down[l - 1].astype(BF16), v_up[l - 1].astype(BF16), row(v_bias[l - 1]))
        r, lw, k, v, kk, b, bonus, sg, yconv = _proj_call(
            x, row(norm_gain[l]), w_in[l].astype(BF16), row(shift_mu[l]), dup, row(decay_bias[l]),
            aup, row(aaa_bias[l]), row(k_k[l]), row(k_a[l]), row(r_k[l]), conv_w[l], hones, vres)
        if l == 0:
            v_first = v
        y_rwkv = _wkv_call(r, lw, k, v, kk, b, bonus, sg, row(ln_gain[l]), row(ln_bias[l]),
                           tri3, gmean)
        fgain = row(final_gain) if l == depth - 1 else None
        x = _out_call(x.reshape(B * T, D_MODEL), y_rwkv.reshape(B * T, RWKV_WIDTH),
                      yconv.reshape(B * T, CONV_WIDTH), w_out[l].astype(BF16),
                      fgain).reshape(B, T, D_MODEL)
    return x
```

```python
import functools

import jax
import jax.numpy as jnp
from jax import lax
from jax.experimental import pallas as pl
from jax.experimental.pallas import tpu as pltpu

D_MODEL = 1024
RWKV_WIDTH = 512
CONV_WIDTH = 512
HEAD_SIZE = 64
CONV_K = 3
D_LORA = 64
RWKV_COLS = 4 * RWKV_WIDTH + 2 * D_LORA
CONV_COLS = 4 * CONV_WIDTH
NORM_EPS = 1e-6
GN_EPS = 1e-5 * HEAD_SIZE
L2_EPS = 1e-12

MXU_DIM = 256
CHUNK = 64
PAIR = 2 * HEAD_SIZE
N_PAIRS = RWKV_WIDTH // PAIR
PROJ_ROWS = 256
WKV_ROWS = 256
WKV_CHUNKS = WKV_ROWS // CHUNK
WKV_ENTRIES = WKV_CHUNKS * N_PAIRS
OUT_ROWS = 512
CARRY_ROWS = 8
VMEM_LIMIT = 56 * 1024 * 1024

F32 = jnp.float32
BF16 = jnp.bfloat16


def _dot(a, b):
    return jnp.dot(a.astype(BF16), b.astype(BF16), preferred_element_type=F32)


def _head_sums(x, hones):
    xb = x.astype(BF16)
    return jnp.concatenate(
        [jnp.dot(xb[:, i:i + MXU_DIM], hones, preferred_element_type=F32)
         for i in range(0, x.shape[1], MXU_DIM)], axis=1)


def _shift_rows(x, carry, n):
    rolled = pltpu.roll(x, n, 0)
    rows = lax.broadcasted_iota(jnp.int32, (CARRY_ROWS, x.shape[1]), 0)
    head = rolled[:CARRY_ROWS]
    for j in range(n):
        head = jnp.where(rows == j, carry[CARRY_ROWS - n + j:CARRY_ROWS - n + j + 1, :], head)
    return jnp.concatenate([head, rolled[CARRY_ROWS:]], axis=0)


def _chunk_running_sum(x):
    rows = lax.broadcasted_iota(jnp.int32, x.shape, 0) % CHUNK
    shift = 1
    while shift < CARRY_ROWS:
        x = x + jnp.where(rows >= shift, pltpu.roll(x, shift, 0), 0.0)
        shift *= 2
    while shift < CHUNK:
        parts = []
        for lo in range(0, x.shape[0], CHUNK):
            parts += [x[lo:lo + shift], x[lo + shift:lo + CHUNK] + x[lo:lo + CHUNK - shift]]
        x = jnp.concatenate(parts, axis=0)
        shift *= 2
    return x


def _sigmoid(x):
    return 0.5 * jnp.tanh(0.5 * x) + 0.5


def _silu(x):
    half = 0.5 * x
    return half * jnp.tanh(half) + half


def _proj_kernel(has_vres, *refs):
    (x_ref, gain_ref, win_ref, mu_ref, dup_ref, dbias_ref, aup_ref, abias_ref, kk_ref, ka_ref,
     rk_ref, convw_ref, hones_ref) = refs[:13]
    pos = 13
    if has_vres:
        vfirst_ref, vdown_ref, vup_ref, vbias_ref = refs[pos:pos + 4]
        pos += 4
    else:
        vfirst_out = refs[pos]
        pos += 1
    (ra_out, aa_out, bb_out, kb_out, be_out, ke_out, v_out, wend_out, bonus_out, sg_out,
     yconv_out) = refs[pos:pos + 11]
    zlast_ref, ulast_ref = refs[pos + 11:]

    @pl.when(pl.program_id(1) == 0)
    def _():
        zlast_ref[...] = jnp.zeros_like(zlast_ref)
        ulast_ref[...] = jnp.zeros_like(ulast_ref)

    x = x_ref[...]
    h = x * lax.rsqrt(jnp.mean(x * x, axis=-1, keepdims=True) + NORM_EPS) * gain_ref[...]
    hb = h.astype(BF16)

    W = RWKV_WIDTH

    def shifted(lo, hi):
        z = jnp.dot(hb, win_ref[:, lo:hi], preferred_element_type=F32)
        prev = _shift_rows(z, zlast_ref[:, lo:hi], 1)
        zlast_ref[:, lo:hi] = z[PROJ_ROWS - CARRY_ROWS:, :]
        return z + (prev - z) * mu_ref[:, lo:hi]

    zwa = shifted(4 * W, RWKV_COLS)
    k = shifted(W, 2 * W)

    lw = -jnp.exp(-0.5) * _sigmoid(dbias_ref[...] + _dot(jnp.tanh(zwa), dup_ref[...]))
    a = _sigmoid(abias_ref[...] + _dot(zwa, aup_ref[...]))
    zc = jnp.dot(hb, win_ref[:, RWKV_COLS:], preferred_element_type=F32)
    cum = _chunk_running_sum(lw)
    e_neg = jnp.exp(-cum)

    hones = hones_ref[...]
    kk = k * kk_ref[...]
    kk = kk * lax.rsqrt(jnp.maximum(_head_sums(kk * kk, hones), L2_EPS * L2_EPS))
    k = k * (1.0 + (a - 1.0) * ka_ref[...])
    b = kk * a
    aa_out[...] = (-kk * jnp.exp(cum - lw)).astype(aa_out.dtype)
    bb_out[...] = (b * e_neg).astype(bb_out.dtype)
    kb_out[...] = (k * e_neg).astype(kb_out.dtype)
    w_end = []
    for c in range(PROJ_ROWS // CHUNK):
        rows = slice(c * CHUNK, (c + 1) * CHUNK)
        cum_end = cum[(c + 1) * CHUNK - 1:(c + 1) * CHUNK, :]
        w_end.append(jnp.exp(cum_end))
        e_end = w_end[c] * e_neg[rows, :]
        be_out[rows, :] = (b[rows, :] * e_end).astype(be_out.dtype)
        ke_out[rows, :] = (k[rows, :] * e_end).astype(ke_out.dtype)
    wend_out[...] = jnp.concatenate(w_end, axis=0)

    r = shifted(0, W)
    v = shifted(2 * W, 3 * W)
    g = shifted(3 * W, 4 * W)

    C = CONV_WIDTH
    gate_b, gate_c, hc, gc = zc[:, :C], zc[:, C:2 * C], zc[:, 2 * C:3 * C], zc[:, 3 * C:]
    u = gate_c * hc
    ulast = ulast_ref[...]
    u1 = _shift_rows(u, ulast, 1)
    u2 = _shift_rows(u, ulast, 2)
    ulast_ref[...] = u[PROJ_ROWS - CARRY_ROWS:, :]
    cw = convw_ref[...]
    conv = u2 * cw[0:1, :] + u1 * cw[1:2, :] + u * cw[2:3, :]
    yconv_out[...] = (gate_b * conv * _silu(gc)).astype(yconv_out.dtype)

    ra_out[...] = (r * jnp.exp(cum)).astype(ra_out.dtype)
    if has_vres:
        mix = _sigmoid(vbias_ref[...] + _dot(_dot(v, vdown_ref[...]), vup_ref[...]))
        v = v + (vfirst_ref[...] - v) * mix
    else:
        vfirst_out[...] = v
    v_out[...] = v.astype(v_out.dtype)
    bonus_out[...] = (_head_sums(r * k * rk_ref[...], hones) * v).astype(bonus_out.dtype)
    sg_out[...] = _silu(g).astype(sg_out.dtype)


def _const_spec(shape):
    return pl.BlockSpec(shape, lambda b, t: (0,) * len(shape))


def _proj_call(x, gain, win, mu, dup, dbias, aup, abias, k_k, k_a, rk, convw, hones, vres):
    B, T, _ = x.shape
    has_vres = vres is not None
    row_spec = lambda width: pl.BlockSpec((None, PROJ_ROWS, width), lambda b, t: (b, t, 0))
    vec = lambda n: _const_spec((1, n))
    in_specs = [row_spec(D_MODEL), vec(D_MODEL), _const_spec(win.shape), vec(RWKV_COLS),
                _const_spec(dup.shape), vec(RWKV_WIDTH), _const_spec(aup.shape), vec(RWKV_WIDTH),
                vec(RWKV_WIDTH), vec(RWKV_WIDTH), vec(RWKV_WIDTH), _const_spec(convw.shape),
                _const_spec(hones.shape)]
    args = [x, gain, win, mu, dup, dbias, aup, abias, k_k, k_a, rk, convw, hones]
    bf16_out = jax.ShapeDtypeStruct((B, T, RWKV_WIDTH), BF16)
    n_chunks = PROJ_ROWS // CHUNK
    wend_spec = pl.BlockSpec((None, None, n_chunks, RWKV_WIDTH), lambda b, t: (b, t, 0, 0))
    wend_shape = jax.ShapeDtypeStruct((B, T // PROJ_ROWS, n_chunks, RWKV_WIDTH), F32)
    out_specs = [row_spec(RWKV_WIDTH)] * 7 + [wend_spec] + [row_spec(RWKV_WIDTH)] * 3
    out_shape = [bf16_out] * 7 + [wend_shape] + [bf16_out] * 3
    if has_vres:
        vfirst, vdown, vup, vbias = vres
        in_specs += [row_spec(RWKV_WIDTH), _const_spec(vdown.shape), _const_spec(vup.shape),
                     vec(RWKV_WIDTH)]
        args += [vfirst, vdown, vup, vbias]
    else:
        out_specs = [row_spec(RWKV_WIDTH)] + out_specs
        out_shape = [jax.ShapeDtypeStruct((B, T, RWKV_WIDTH), F32)] + out_shape
    outs = pl.pallas_call(
        functools.partial(_proj_kernel, has_vres),
        grid=(B, T // PROJ_ROWS),
        in_specs=in_specs,
        out_specs=out_specs,
        out_shape=out_shape,
        scratch_shapes=[pltpu.VMEM((CARRY_ROWS, RWKV_COLS), F32),
                        pltpu.VMEM((CARRY_ROWS, CONV_WIDTH), F32)],
        compiler_params=pltpu.CompilerParams(
            dimension_semantics=("parallel", "arbitrary"), vmem_limit_bytes=VMEM_LIMIT),
    )(*args)
    if has_vres:
        return None, outs
    return outs[0], outs[1:]


def _wkv_kernel(ra_ref, aa_ref, bb_ref, kb_ref, be_ref, ke_ref, v_ref, wend_ref, bonus_ref, sg_ref,
                lng_ref, lnb_ref, gmean_ref, y_out, state_ref, gm_ref, hn_ref):
    @pl.when(pl.program_id(1) == 0)
    def _():
        state_ref[...] = jnp.zeros_like(state_ref)
        gm_ref[...] = jnp.zeros_like(gm_ref)
        hn_ref[...] = jnp.zeros_like(hn_ref)

    lane = lax.broadcasted_iota(jnp.int32, (CHUNK, PAIR), 1)
    row = lax.broadcasted_iota(jnp.int32, (CHUNK, PAIR), 0)
    head0 = lane < HEAD_SIZE
    col = jnp.where(head0, lane, lane - HEAD_SIZE)
    strict = col < row
    incl = col <= row
    eye_mask = col == row

    def bd(m):
        return jnp.concatenate([jnp.where(head0, m, 0.0), jnp.where(head0, 0.0, m)], axis=0)

    gmean = gmean_ref[...]
    zeros_pp = jnp.zeros((PAIR, PAIR), F32)

    each = lambda f, *cols: [f(*xs) for xs in zip(*cols)]
    tiles = [(pl.ds(c * CHUNK, CHUNK), pl.ds(p * PAIR, PAIR))
             for c in range(WKV_CHUNKS) for p in range(N_PAIRS)]
    load = lambda ref: [ref[rw, ln].astype(F32) for rw, ln in tiles]

    state = [state_ref[p] for p in range(N_PAIRS)]
    y = []

    def recurrence(c):
        for p in range(N_PAIRS):
            i = c * N_PAIRS + p
            both = jnp.dot(gm_ref[i], bd(state[p]).astype(BF16), preferred_element_type=F32)
            hn = hn_ref[i]
            y.append(both[:CHUNK] + hn[:CHUNK])
            state[p] = both[CHUNK:] + hn[CHUNK:]

    ra, aa, bb, kb, be, ke, v = (load(ref) for ref in
                                 (ra_ref, aa_ref, bb_ref, kb_ref, be_ref, ke_ref, v_ref))
    w_end = [wend_ref[c:c + 1, pl.ds(p * PAIR, PAIR)]
             for c in range(WKV_CHUNKS) for p in range(N_PAIRS)]
    gram = each(lambda a_, r_, b_, k_: lax.dot_general(
        jnp.concatenate([a_, r_], axis=0).astype(BF16),
        jnp.concatenate([bd(b_), bd(k_)], axis=0).astype(BF16),
        (((1,), (1,)), ((), ())), preferred_element_type=F32), aa, ra, bb, kb)
    n_ab = [jnp.where(strict, g[:CHUNK, :PAIR], 0.0) for g in gram]
    a_ak = [jnp.where(strict, g[:CHUNK, PAIR:], 0.0) for g in gram]
    a_rb = [jnp.where(incl, g[CHUNK:, :PAIR], 0.0) for g in gram]
    a_rk = [jnp.where(incl, g[CHUNK:, PAIR:], 0.0) for g in gram]
    recurrence(0)

    tinv = [jnp.where(eye_mask, 1.0, n) for n in n_ab]
    power = each(lambda n: _dot(n, bd(n)), n_ab)
    for j in range(4):
        if j + 1 < WKV_CHUNKS:
            recurrence(j + 1)
        both = each(lambda p, t: _dot(jnp.concatenate([p, t], axis=0), bd(p)), power, tinv)
        power = [x[:CHUNK] for x in both]
        tinv = each(lambda t, x: t + x[CHUNK:], tinv, both)
    assert len(y) == WKV_ENTRIES
    for p in range(N_PAIRS):
        state_ref[p] = state[p]
    tinv = each(lambda t, p: t + _dot(t, bd(p)), tinv, power)

    y = jnp.concatenate(y, axis=0)
    y_hi = y.astype(BF16)
    y_lo = (y - y_hi.astype(F32)).astype(BF16)
    d = y - jnp.dot(jnp.concatenate([y_hi, y_lo], axis=1), gmean, preferred_element_type=F32)
    var = jnp.dot((d * d).astype(BF16), gmean[:PAIR], preferred_element_type=F32)
    yn = d * lax.rsqrt(var + GN_EPS)
    for i, (rw, ln) in enumerate(tiles):
        yi = yn[i * CHUNK:(i + 1) * CHUNK] * lng_ref[:, ln] + lnb_ref[:, ln]
        y_out[rw, ln] = ((yi + bonus_ref[rw, ln].astype(F32))
                         * sg_ref[rw, ln].astype(F32)).astype(y_out.dtype)

    akv = each(lambda a_, v_: _dot(a_, bd(v_)), a_ak, v)
    pq = each(lambda t, a_, q: _dot(t, jnp.concatenate([bd(a_), bd(q)], axis=1)), tinv, aa, akv)

    def fold_t(m):
        t = bd(m).T
        return t[:CHUNK] + t[CHUNK:]

    ghmn = each(lambda rb, rk, b_, k_, x, v_: _dot(
        jnp.concatenate([jnp.concatenate([rb, rk], axis=1),
                         jnp.concatenate([fold_t(b_), fold_t(k_)], axis=1)], axis=0),
        jnp.concatenate([jnp.concatenate([bd(x[:, :PAIR]), bd(x[:, PAIR:])], axis=1),
                         jnp.concatenate([zeros_pp, bd(v_)], axis=1)], axis=0)),
        a_rb, a_rk, be, ke, pq, v)
    for i in range(WKV_ENTRIES):
        gm_ref[i] = jnp.concatenate(
            [ghmn[i][:CHUNK, :PAIR] + ra[i],
             ghmn[i][CHUNK:, :PAIR] + jnp.where(eye_mask, w_end[i], 0.0)], axis=0).astype(BF16)
        hn_ref[i] = ghmn[i][:, PAIR:]


def _wkv_call(ops, wend, bonus, sg, lng, lnb, gmean):
    B, T, _ = bonus.shape
    n_t = T // WKV_ROWS
    now = lambda bi, t: (bi, jnp.minimum(t, n_t - 1), 0)
    lagged = lambda bi, t: (bi, jnp.maximum(t - 1, 0), 0)
    blk = lambda index_map: pl.BlockSpec((None, WKV_ROWS, RWKV_WIDTH), index_map)
    wend_spec = pl.BlockSpec((None, None, WKV_CHUNKS, RWKV_WIDTH),
                             lambda bi, t: (bi, jnp.minimum(t, n_t - 1), 0, 0))
    const = lambda a: pl.BlockSpec(a.shape, lambda bi, t: (0, 0))
    return pl.pallas_call(
        _wkv_kernel,
        grid=(B, n_t + 1),
        in_specs=[blk(now)] * 7 + [wend_spec, blk(lagged), blk(lagged),
                                   const(lng), const(lnb), const(gmean)],
        out_specs=blk(lagged),
        out_shape=jax.ShapeDtypeStruct((B, T, RWKV_WIDTH), BF16),
        scratch_shapes=[pltpu.VMEM((N_PAIRS, CHUNK, PAIR), F32),
                        pltpu.VMEM((WKV_ENTRIES, 2 * CHUNK, PAIR), BF16),
                        pltpu.VMEM((WKV_ENTRIES, 2 * CHUNK, PAIR), F32)],
        compiler_params=pltpu.CompilerParams(
            dimension_semantics=("parallel", "arbitrary"), vmem_limit_bytes=VMEM_LIMIT),
    )(*ops, wend, bonus, sg, lng, lnb, gmean)


def _out_kernel(final, *refs):
    if final:
        x_ref, yr_ref, yc_ref, wout_ref, fgain_ref, o_ref = refs
    else:
        x_ref, yr_ref, yc_ref, wout_ref, o_ref = refs
    y = (jnp.dot(yr_ref[...], wout_ref[:RWKV_WIDTH, :], preferred_element_type=F32)
         + jnp.dot(yc_ref[...], wout_ref[RWKV_WIDTH:, :], preferred_element_type=F32))
    x = x_ref[...] + y
    if final:
        x = x * lax.rsqrt(jnp.mean(x * x, axis=-1, keepdims=True) + NORM_EPS) * fgain_ref[...]
    o_ref[...] = x


def _out_call(x2, yr2, yc2, wout, fgain):
    n = x2.shape[0]
    final = fgain is not None
    rows = lambda width: pl.BlockSpec((OUT_ROWS, width), lambda i: (i, 0))
    in_specs = [rows(D_MODEL), rows(RWKV_WIDTH), rows(CONV_WIDTH),
                pl.BlockSpec(wout.shape, lambda i: (0, 0))]
    args = [x2, yr2, yc2, wout]
    if final:
        in_specs.append(pl.BlockSpec((1, D_MODEL), lambda i: (0, 0)))
        args.append(fgain)
    return pl.pallas_call(
        functools.partial(_out_kernel, final),
        grid=(n // OUT_ROWS,),
        in_specs=in_specs,
        out_specs=rows(D_MODEL),
        out_shape=jax.ShapeDtypeStruct((n, D_MODEL), F32),
        compiler_params=pltpu.CompilerParams(
            dimension_semantics=("parallel",), vmem_limit_bytes=VMEM_LIMIT),
    )(*args)


def kernel(x, norm_gain, w_in, shift_mu, decay_up, decay_bias, aaa_up, aaa_bias, k_k, k_a, r_k,
           ln_gain, ln_bias, v_down, v_up, v_bias, conv_w, w_out, final_gain):
    depth = w_in.shape[0]
    B, T, _ = x.shape
    assert T % WKV_ROWS == 0 and T % PROJ_ROWS == 0 and (B * T) % OUT_ROWS == 0
    assert PROJ_ROWS % CHUNK == 0 and HEAD_SIZE == CHUNK

    idx = jnp.arange(MXU_DIM)
    hones = (idx[:, None] // HEAD_SIZE == idx[None, :] // HEAD_SIZE).astype(BF16)
    pidx = jnp.arange(PAIR)
    gmean = ((pidx[:, None] // HEAD_SIZE == pidx[None, :] // HEAD_SIZE).astype(F32)
             / HEAD_SIZE).astype(BF16)
    gmean = jnp.concatenate([gmean, gmean], axis=0)
    lora_pad = jnp.zeros((D_LORA, RWKV_WIDTH), F32)

    row = lambda a: a.reshape(1, -1)
    v_first = None
    for l in range(depth):
        dup = jnp.concatenate([decay_up[l], lora_pad], axis=0).astype(BF16)
        aup = jnp.concatenate([lora_pad, aaa_up[l]], axis=0).astype(BF16)
        vres = None
        if l > 0:
            vres = (v_first, v_down[l - 1].astype(BF16), v_up[l - 1].astype(BF16), row(v_bias[l - 1]))
        new_v_first, outs = _proj_call(
            x, row(norm_gain[l]), w_in[l].astype(BF16), row(shift_mu[l]), dup, row(decay_bias[l]),
            aup, row(aaa_bias[l]), row(k_k[l]), row(k_a[l]), row(r_k[l]), conv_w[l], hones, vres)
        if l == 0:
            v_first = new_v_first
        ops, wend, bonus, sg, yconv = outs[:7], outs[7], outs[8], outs[9], outs[10]
        wend = wend.reshape(B, T // WKV_ROWS, WKV_CHUNKS, RWKV_WIDTH)
        y_rwkv = _wkv_call(ops, wend, bonus, sg, row(ln_gain[l]), row(ln_bias[l]), gmean)
        fgain = row(final_gain) if l == depth - 1 else None
        x = _out_call(x.reshape(B * T, D_MODEL), y_rwkv.reshape(B * T, RWKV_WIDTH),
                      yconv.reshape(B * T, CONV_WIDTH), w_out[l].astype(BF16),
                      fgain).reshape(B, T, D_MODEL)
    return x
```

```python
import functools

import jax
import jax.numpy as jnp
from jax import lax
from jax.experimental import pallas as pl
from jax.experimental.pallas import tpu as pltpu

D_MODEL = 1024
RWKV_WIDTH = 512
CONV_WIDTH = 512
HEAD_SIZE = 64
CONV_K = 3
D_LORA = 64
RWKV_COLS = 4 * RWKV_WIDTH + 2 * D_LORA
CONV_COLS = 4 * CONV_WIDTH
NORM_EPS = 1e-6
GN_EPS = 1e-5 * HEAD_SIZE
L2_EPS = 1e-12

MXU_DIM = 256
CHUNK = 64
PAIR = 2 * HEAD_SIZE
N_PAIRS = RWKV_WIDTH // PAIR
PROJ_ROWS = 256
WKV_ROWS = 256
WKV_CHUNKS = WKV_ROWS // CHUNK
WKV_ENTRIES = WKV_CHUNKS * N_PAIRS
CARRY_ROWS = 8
VMEM_LIMIT = 56 * 1024 * 1024

F32 = jnp.float32
BF16 = jnp.bfloat16


def _dot(a, b):
    return jnp.dot(a.astype(BF16), b.astype(BF16), preferred_element_type=F32)


def _head_sums(x, hones):
    xb = x.astype(BF16)
    return jnp.concatenate(
        [jnp.dot(xb[:, i:i + MXU_DIM], hones, preferred_element_type=F32)
         for i in range(0, x.shape[1], MXU_DIM)], axis=1)


def _shift_rows(x, carry, n):
    rolled = pltpu.roll(x, n, 0)
    rows = lax.broadcasted_iota(jnp.int32, (CARRY_ROWS, x.shape[1]), 0)
    head = rolled[:CARRY_ROWS]
    for j in range(n):
        head = jnp.where(rows == j, carry[CARRY_ROWS - n + j:CARRY_ROWS - n + j + 1, :], head)
    return jnp.concatenate([head, rolled[CARRY_ROWS:]], axis=0)


def _chunk_running_sum(x):
    rows = lax.broadcasted_iota(jnp.int32, x.shape, 0) % CHUNK
    shift = 1
    while shift < CARRY_ROWS:
        x = x + jnp.where(rows >= shift, pltpu.roll(x, shift, 0), 0.0)
        shift *= 2
    while shift < CHUNK:
        parts = []
        for lo in range(0, x.shape[0], CHUNK):
            parts += [x[lo:lo + shift], x[lo + shift:lo + CHUNK] + x[lo:lo + CHUNK - shift]]
        x = jnp.concatenate(parts, axis=0)
        shift *= 2
    return x


def _sigmoid(x):
    return 0.5 * jnp.tanh(0.5 * x) + 0.5


def _silu(x):
    half = 0.5 * x
    return half * jnp.tanh(half) + half


def _proj_kernel(has_vres, *refs):
    (x_ref, gain_ref, win_ref, mu_ref, dup_ref, dbias_ref, aup_ref, abias_ref, kk_ref, ka_ref,
     rk_ref, convw_ref, hones_ref) = refs[:13]
    pos = 13
    if has_vres:
        vfirst_ref, vdown_ref, vup_ref, vbias_ref = refs[pos:pos + 4]
        pos += 4
    else:
        vfirst_out = refs[pos]
        pos += 1
    (ra_out, aa_out, bb_out, kb_out, be_out, ke_out, v_out, wend_out, bonus_out, sg_out,
     yconv_out) = refs[pos:pos + 11]
    zlast_ref, ulast_ref = refs[pos + 11:]

    @pl.when(pl.program_id(1) == 0)
    def _():
        zlast_ref[...] = jnp.zeros_like(zlast_ref)
        ulast_ref[...] = jnp.zeros_like(ulast_ref)

    x = x_ref[...]
    h = x * lax.rsqrt(jnp.mean(x * x, axis=-1, keepdims=True) + NORM_EPS) * gain_ref[...]
    hb = h.astype(BF16)

    W = RWKV_WIDTH

    def shifted(lo, hi):
        z = jnp.dot(hb, win_ref[:, lo:hi], preferred_element_type=F32)
        prev = _shift_rows(z, zlast_ref[:, lo:hi], 1)
        zlast_ref[:, lo:hi] = z[PROJ_ROWS - CARRY_ROWS:, :]
        return z + (prev - z) * mu_ref[:, lo:hi]

    zwa = shifted(4 * W, RWKV_COLS)
    k = shifted(W, 2 * W)

    lw = -jnp.exp(-0.5) * _sigmoid(dbias_ref[...] + _dot(jnp.tanh(zwa), dup_ref[...]))
    a = _sigmoid(abias_ref[...] + _dot(zwa, aup_ref[...]))
    zc = jnp.dot(hb, win_ref[:, RWKV_COLS:], preferred_element_type=F32)
    cum = _chunk_running_sum(lw)
    e_neg = jnp.exp(-cum)

    hones = hones_ref[...]
    kk = k * kk_ref[...]
    kk = kk * lax.rsqrt(jnp.maximum(_head_sums(kk * kk, hones), L2_EPS * L2_EPS))
    k = k * (1.0 + (a - 1.0) * ka_ref[...])
    b = kk * a
    aa_out[...] = (-kk * jnp.exp(cum - lw)).astype(aa_out.dtype)
    bb_out[...] = (b * e_neg).astype(bb_out.dtype)
    kb_out[...] = (k * e_neg).astype(kb_out.dtype)
    w_end = []
    for c in range(PROJ_ROWS // CHUNK):
        rows = slice(c * CHUNK, (c + 1) * CHUNK)
        cum_end = cum[(c + 1) * CHUNK - 1:(c + 1) * CHUNK, :]
        w_end.append(jnp.exp(cum_end))
        e_end = w_end[c] * e_neg[rows, :]
        be_out[rows, :] = (b[rows, :] * e_end).astype(be_out.dtype)
        ke_out[rows, :] = (k[rows, :] * e_end).astype(ke_out.dtype)
    wend_out[...] = jnp.concatenate(w_end, axis=0)

    r = shifted(0, W)
    v = shifted(2 * W, 3 * W)
    g = shifted(3 * W, 4 * W)

    C = CONV_WIDTH
    gate_b, gate_c, hc, gc = zc[:, :C], zc[:, C:2 * C], zc[:, 2 * C:3 * C], zc[:, 3 * C:]
    u = gate_c * hc
    ulast = ulast_ref[...]
    u1 = _shift_rows(u, ulast, 1)
    u2 = _shift_rows(u, ulast, 2)
    ulast_ref[...] = u[PROJ_ROWS - CARRY_ROWS:, :]
    cw = convw_ref[...]
    conv = u2 * cw[0:1, :] + u1 * cw[1:2, :] + u * cw[2:3, :]
    yconv_out[...] = (gate_b * conv * _silu(gc)).astype(yconv_out.dtype)

    ra_out[...] = (r * jnp.exp(cum)).astype(ra_out.dtype)
    if has_vres:
        mix = _sigmoid(vbias_ref[...] + _dot(_dot(v, vdown_ref[...]), vup_ref[...]))
        v = v + (vfirst_ref[...] - v) * mix
    else:
        vfirst_out[...] = v
    v_out[...] = v.astype(v_out.dtype)
    bonus_out[...] = (_head_sums(r * k * rk_ref[...], hones) * v).astype(bonus_out.dtype)
    sg_out[...] = _silu(g).astype(sg_out.dtype)


def _proj_call(x, gain, win, mu, dup, dbias, aup, abias, k_k, k_a, rk, convw, hones, vres):
    B, T, _ = x.shape
    has_vres = vres is not None
    index_map = lambda b, t: (b, t, 0)
    rows = lambda width: pl.BlockSpec((None, PROJ_ROWS, width), index_map)
    const = lambda a: pl.BlockSpec(a.shape, lambda b, t: (0,) * a.ndim)
    consts = [gain, win, mu, dup, dbias, aup, abias, k_k, k_a, rk, convw, hones]
    in_specs = [rows(D_MODEL)] + [const(a) for a in consts]
    args = [x] + consts
    bf16_out = jax.ShapeDtypeStruct((B, T, RWKV_WIDTH), BF16)
    n_chunks = PROJ_ROWS // CHUNK
    wend_spec = pl.BlockSpec((None, None, n_chunks, RWKV_WIDTH), lambda b, t: (b, t, 0, 0))
    wend_shape = jax.ShapeDtypeStruct((B, T // PROJ_ROWS, n_chunks, RWKV_WIDTH), F32)
    out_specs = [rows(RWKV_WIDTH)] * 7 + [wend_spec] + [rows(RWKV_WIDTH)] * 3
    out_shape = [bf16_out] * 7 + [wend_shape] + [bf16_out] * 3
    if has_vres:
        vfirst, vdown, vup, vbias = vres
        in_specs += [rows(RWKV_WIDTH), const(vdown), const(vup), const(vbias)]
        args += [vfirst, vdown, vup, vbias]
    else:
        out_specs = [rows(RWKV_WIDTH)] + out_specs
        out_shape = [jax.ShapeDtypeStruct((B, T, RWKV_WIDTH), F32)] + out_shape
    outs = pl.pallas_call(
        functools.partial(_proj_kernel, has_vres),
        grid=(B, T // PROJ_ROWS),
        in_specs=in_specs,
        out_specs=out_specs,
        out_shape=out_shape,
        scratch_shapes=[pltpu.VMEM((CARRY_ROWS, RWKV_COLS), F32),
                        pltpu.VMEM((CARRY_ROWS, CONV_WIDTH), F32)],
        compiler_params=pltpu.CompilerParams(
            dimension_semantics=("parallel", "arbitrary"), vmem_limit_bytes=VMEM_LIMIT),
    )(*args)
    if has_vres:
        return None, outs
    return outs[0], outs[1:]


def _wkv_kernel(final, ra_ref, aa_ref, bb_ref, kb_ref, be_ref, ke_ref, v_ref, wend_ref, bonus_ref,
                sg_ref, x_ref, yconv_ref, lng_ref, lnb_ref, gmean_ref, wout_ref, *refs):
    if final:
        fgain_ref, x_out, state_ref, gm_ref, hn_ref = refs
    else:
        x_out, state_ref, gm_ref, hn_ref = refs
    @pl.when(pl.program_id(1) == 0)
    def _():
        state_ref[...] = jnp.zeros_like(state_ref)
        gm_ref[...] = jnp.zeros_like(gm_ref)
        hn_ref[...] = jnp.zeros_like(hn_ref)

    lane = lax.broadcasted_iota(jnp.int32, (CHUNK, PAIR), 1)
    row = lax.broadcasted_iota(jnp.int32, (CHUNK, PAIR), 0)
    head0 = lane < HEAD_SIZE
    col = jnp.where(head0, lane, lane - HEAD_SIZE)
    strict = col < row
    incl = col <= row
    eye_mask = col == row

    def bd(m):
        return jnp.concatenate([jnp.where(head0, m, 0.0), jnp.where(head0, 0.0, m)], axis=0)

    gmean = gmean_ref[...]
    zeros_pp = jnp.zeros((PAIR, PAIR), F32)

    each = lambda f, *cols: [f(*xs) for xs in zip(*cols)]
    tiles = [(pl.ds(c * CHUNK, CHUNK), pl.ds(p * PAIR, PAIR))
             for c in range(WKV_CHUNKS) for p in range(N_PAIRS)]
    load = lambda ref: [ref[rw, ln].astype(F32) for rw, ln in tiles]

    state = [state_ref[p] for p in range(N_PAIRS)]
    y = []

    def recurrence(c):
        for p in range(N_PAIRS):
            i = c * N_PAIRS + p
            both = jnp.dot(gm_ref[i], bd(state[p]).astype(BF16), preferred_element_type=F32)
            hn = hn_ref[i]
            y.append(both[:CHUNK] + hn[:CHUNK])
            state[p] = both[CHUNK:] + hn[CHUNK:]

    ra, aa, bb, kb, be, ke, v = (load(ref) for ref in
                                 (ra_ref, aa_ref, bb_ref, kb_ref, be_ref, ke_ref, v_ref))
    w_end = [wend_ref[c:c + 1, pl.ds(p * PAIR, PAIR)]
             for c in range(WKV_CHUNKS) for p in range(N_PAIRS)]
    gram = each(lambda a_, r_, b_, k_: lax.dot_general(
        jnp.concatenate([a_, r_], axis=0).astype(BF16),
        jnp.concatenate([bd(b_), bd(k_)], axis=0).astype(BF16),
        (((1,), (1,)), ((), ())), preferred_element_type=F32), aa, ra, bb, kb)
    n_ab = [jnp.where(strict, g[:CHUNK, :PAIR], 0.0) for g in gram]
    a_ak = [jnp.where(strict, g[:CHUNK, PAIR:], 0.0) for g in gram]
    a_rb = [jnp.where(incl, g[CHUNK:, :PAIR], 0.0) for g in gram]
    a_rk = [jnp.where(incl, g[CHUNK:, PAIR:], 0.0) for g in gram]
    recurrence(0)

    tinv = [jnp.where(eye_mask, 1.0, n) for n in n_ab]
    power = each(lambda n: _dot(n, bd(n)), n_ab)

    def inverse_step():
        nonlocal power, tinv
        both = each(lambda p, t: _dot(jnp.concatenate([p, t], axis=0), bd(p)), power, tinv)
        power = [x[:CHUNK] for x in both]
        tinv = each(lambda t, x: t + x[CHUNK:], tinv, both)

    for c in range(1, WKV_CHUNKS):
        recurrence(c)
        inverse_step()
    for p in range(N_PAIRS):
        state_ref[p] = state[p]

    y = jnp.concatenate(y, axis=0)
    y_hi = y.astype(BF16)
    y_lo = (y - y_hi.astype(F32)).astype(BF16)
    d = y - jnp.dot(jnp.concatenate([y_hi, y_lo], axis=1), gmean, preferred_element_type=F32)
    for _ in range(WKV_CHUNKS - 1, 4):
        inverse_step()
    var = jnp.dot((d * d).astype(BF16), gmean[:PAIR], preferred_element_type=F32)
    tinv = each(lambda t, p: t + _dot(t, bd(p)), tinv, power)
    yn = d * lax.rsqrt(var + GN_EPS)
    gated = []
    for i, (rw, ln) in enumerate(tiles):
        yi = yn[i * CHUNK:(i + 1) * CHUNK] * lng_ref[:, ln] + lnb_ref[:, ln]
        gated.append(((yi + bonus_ref[rw, ln].astype(F32))
                      * sg_ref[rw, ln].astype(F32)).astype(BF16))
    y_rwkv = jnp.concatenate(
        [jnp.concatenate(gated[c * N_PAIRS:(c + 1) * N_PAIRS], axis=1) for c in range(WKV_CHUNKS)],
        axis=0)

    akv = each(lambda a_, v_: _dot(a_, bd(v_)), a_ak, v)

    x_new = (x_ref[...]
             + jnp.dot(y_rwkv, wout_ref[:RWKV_WIDTH, :], preferred_element_type=F32)
             + jnp.dot(yconv_ref[...], wout_ref[RWKV_WIDTH:, :], preferred_element_type=F32))
    if final:
        x_new = (x_new * lax.rsqrt(jnp.mean(x_new * x_new, axis=-1, keepdims=True) + NORM_EPS)
                 * fgain_ref[...])
    x_out[...] = x_new

    pq = each(lambda t, a_, q: _dot(t, jnp.concatenate([bd(a_), bd(q)], axis=1)), tinv, aa, akv)

    def fold_t(m):
        t = bd(m).T
        return t[:CHUNK] + t[CHUNK:]

    ghmn = each(lambda rb, rk, b_, k_, x, v_: _dot(
        jnp.concatenate([jnp.concatenate([rb, rk], axis=1),
                         jnp.concatenate([fold_t(b_), fold_t(k_)], axis=1)], axis=0),
        jnp.concatenate([jnp.concatenate([bd(x[:, :PAIR]), bd(x[:, PAIR:])], axis=1),
                         jnp.concatenate([zeros_pp, bd(v_)], axis=1)], axis=0)),
        a_rb, a_rk, be, ke, pq, v)
    for i in range(WKV_ENTRIES):
        gm_ref[i] = jnp.concatenate(
            [ghmn[i][:CHUNK, :PAIR] + ra[i],
             ghmn[i][CHUNK:, :PAIR] + jnp.where(eye_mask, w_end[i], 0.0)], axis=0).astype(BF16)
        hn_ref[i] = ghmn[i][:, PAIR:]


def _wkv_call(ops, wend, bonus, sg, x, yconv, lng, lnb, gmean, wout, fgain):
    B, T, _ = x.shape
    n_t = T // WKV_ROWS
    final = fgain is not None
    now = lambda bi, t: (bi, jnp.minimum(t, n_t - 1), 0)
    lagged = lambda bi, t: (bi, jnp.maximum(t - 1, 0), 0)
    blk = lambda width, index_map: pl.BlockSpec((None, WKV_ROWS, width), index_map)
    wend_spec = pl.BlockSpec((None, None, WKV_CHUNKS, RWKV_WIDTH),
                             lambda bi, t: (bi, jnp.minimum(t, n_t - 1), 0, 0))
    const = lambda a: pl.BlockSpec(a.shape, lambda bi, t: (0, 0))
    in_specs = ([blk(RWKV_WIDTH, now)] * 7
                + [wend_spec, blk(RWKV_WIDTH, lagged), blk(RWKV_WIDTH, lagged),
                   blk(D_MODEL, lagged), blk(CONV_WIDTH, lagged),
                   const(lng), const(lnb), const(gmean), const(wout)])
    args = [*ops, wend, bonus, sg, x, yconv, lng, lnb, gmean, wout]
    if final:
        in_specs.append(const(fgain))
        args.append(fgain)
    return pl.pallas_call(
        functools.partial(_wkv_kernel, final),
        grid=(B, n_t + 1),
        in_specs=in_specs,
        out_specs=blk(D_MODEL, lagged),
        out_shape=jax.ShapeDtypeStruct((B, T, D_MODEL), F32),
        scratch_shapes=[pltpu.VMEM((N_PAIRS, CHUNK, PAIR), F32),
                        pltpu.VMEM((WKV_ENTRIES, 2 * CHUNK, PAIR), BF16),
                        pltpu.VMEM((WKV_ENTRIES, 2 * CHUNK, PAIR), F32)],
        compiler_params=pltpu.CompilerParams(
            dimension_semantics=("parallel", "arbitrary"), vmem_limit_bytes=VMEM_LIMIT),
    )(*args)


def kernel(x, norm_gain, w_in, shift_mu, decay_up, decay_bias, aaa_up, aaa_bias, k_k, k_a, r_k,
           ln_gain, ln_bias, v_down, v_up, v_bias, conv_w, w_out, final_gain):
    depth = w_in.shape[0]
    B, T, _ = x.shape
    assert T % WKV_ROWS == 0 and T % PROJ_ROWS == 0
    assert PROJ_ROWS % CHUNK == 0 and HEAD_SIZE == CHUNK

    idx = jnp.arange(MXU_DIM)
    hones = (idx[:, None] // HEAD_SIZE == idx[None, :] // HEAD_SIZE).astype(BF16)
    pidx = jnp.arange(PAIR)
    gmean = ((pidx[:, None] // HEAD_SIZE == pidx[None, :] // HEAD_SIZE).astype(F32)
             / HEAD_SIZE).astype(BF16)
    gmean = jnp.concatenate([gmean, gmean], axis=0)
    lora_pad = jnp.zeros((D_LORA, RWKV_WIDTH), F32)

    row = lambda a: a.reshape(1, -1)
    v_first = None
    for l in range(depth):
        dup = jnp.concatenate([decay_up[l], lora_pad], axis=0).astype(BF16)
        aup = jnp.concatenate([lora_pad, aaa_up[l]], axis=0).astype(BF16)
        vres = None
        if l > 0:
            vres = (v_first, v_down[l - 1].astype(BF16), v_up[l - 1].astype(BF16), row(v_bias[l - 1]))
        new_v_first, outs = _proj_call(
            x, row(norm_gain[l]), w_in[l].astype(BF16), row(shift_mu[l]), dup, row(decay_bias[l]),
            aup, row(aaa_bias[l]), row(k_k[l]), row(k_a[l]), row(r_k[l]), conv_w[l], hones, vres)
        if l == 0:
            v_first = new_v_first
        ops, wend, bonus, sg, yconv = outs[:7], outs[7], outs[8], outs[9], outs[10]
        wend = wend.reshape(B, T // WKV_ROWS, WKV_CHUNKS, RWKV_WIDTH)
        fgain = row(final_gain) if l == depth - 1 else None
        x = _wkv_call(ops, wend, bonus, sg, x, yconv, row(ln_gain[l]), row(ln_bias[l]), gmean,
                      w_out[l].astype(BF16), fgain)
    return x
```

```python
import functools

import jax
import jax.numpy as jnp
from jax import lax
from jax.experimental import pallas as pl
from jax.experimental.pallas import tpu as pltpu

D_MODEL = 1024
RWKV_WIDTH = 512
CONV_WIDTH = 512
HEAD_SIZE = 64
CONV_K = 3
D_LORA = 64
RWKV_COLS = 4 * RWKV_WIDTH + 2 * D_LORA
CONV_COLS = 4 * CONV_WIDTH
NORM_EPS = 1e-6
GN_EPS = 1e-5 * HEAD_SIZE
L2_EPS = 1e-12

MXU_DIM = 256
CHUNK = 64
PAIR = 2 * HEAD_SIZE
N_PAIRS = RWKV_WIDTH // PAIR
PROJ_ROWS = 512
WKV_ROWS = 256
WKV_CHUNKS = WKV_ROWS // CHUNK
WKV_ENTRIES = WKV_CHUNKS * N_PAIRS
CARRY_ROWS = 8
VMEM_LIMIT = 56 * 1024 * 1024

F32 = jnp.float32
BF16 = jnp.bfloat16


def _dot(a, b):
    return jnp.dot(a.astype(BF16), b.astype(BF16), preferred_element_type=F32)


def _head_sums(x, hones):
    xb = x.astype(BF16)
    return jnp.concatenate(
        [jnp.dot(xb[:, i:i + MXU_DIM], hones, preferred_element_type=F32)
         for i in range(0, x.shape[1], MXU_DIM)], axis=1)


def _shift_rows(x, carry, n):
    rolled = pltpu.roll(x, n, 0)
    rows = lax.broadcasted_iota(jnp.int32, (CARRY_ROWS, x.shape[1]), 0)
    head = rolled[:CARRY_ROWS]
    for j in range(n):
        head = jnp.where(rows == j, carry[CARRY_ROWS - n + j:CARRY_ROWS - n + j + 1, :], head)
    return jnp.concatenate([head, rolled[CARRY_ROWS:]], axis=0)


def _chunk_running_sum(x):
    rows = lax.broadcasted_iota(jnp.int32, x.shape, 0) % CHUNK
    shift = 1
    while shift < CARRY_ROWS:
        x = x + jnp.where(rows >= shift, pltpu.roll(x, shift, 0), 0.0)
        shift *= 2
    while shift < CHUNK:
        parts = []
        for lo in range(0, x.shape[0], CHUNK):
            parts += [x[lo:lo + shift], x[lo + shift:lo + CHUNK] + x[lo:lo + CHUNK - shift]]
        x = jnp.concatenate(parts, axis=0)
        shift *= 2
    return x


def _sigmoid(x):
    return 0.5 * jnp.tanh(0.5 * x) + 0.5


def _silu(x):
    half = 0.5 * x
    return half * jnp.tanh(half) + half


def _proj_kernel(has_vres, *refs):
    (x_ref, gain_ref, win_ref, mu_ref, dup_ref, dbias_ref, aup_ref, abias_ref, kk_ref, ka_ref,
     rk_ref, convw_ref, hones_ref) = refs[:13]
    pos = 13
    if has_vres:
        vfirst_ref, vdown_ref, vup_ref, vbias_ref = refs[pos:pos + 4]
        pos += 4
    else:
        vfirst_out = refs[pos]
        pos += 1
    (ra_out, aa_out, bb_out, kb_out, be_out, ke_out, v_out, wend_out, bonus_out, sg_out,
     yconv_out) = refs[pos:pos + 11]
    zlast_ref, ulast_ref = refs[pos + 11:]

    @pl.when(pl.program_id(1) == 0)
    def _():
        zlast_ref[...] = jnp.zeros_like(zlast_ref)
        ulast_ref[...] = jnp.zeros_like(ulast_ref)

    x = x_ref[...]
    h = x * lax.rsqrt(jnp.mean(x * x, axis=-1, keepdims=True) + NORM_EPS) * gain_ref[...]
    hb = h.astype(BF16)

    W = RWKV_WIDTH

    def shifted(lo, hi):
        z = jnp.dot(hb, win_ref[:, lo:hi], preferred_element_type=F32)
        prev = _shift_rows(z, zlast_ref[:, lo:hi], 1)
        zlast_ref[:, lo:hi] = z[PROJ_ROWS - CARRY_ROWS:, :]
        return z + (prev - z) * mu_ref[:, lo:hi]

    zwa = shifted(4 * W, RWKV_COLS)
    k = shifted(W, 2 * W)

    lw = -jnp.exp(-0.5) * _sigmoid(dbias_ref[...] + _dot(jnp.tanh(zwa), dup_ref[...]))
    a = _sigmoid(abias_ref[...] + _dot(zwa, aup_ref[...]))
    zc = jnp.dot(hb, win_ref[:, RWKV_COLS:], preferred_element_type=F32)
    cum = _chunk_running_sum(lw)
    e_neg = jnp.exp(-cum)

    hones = hones_ref[...]
    kk = k * kk_ref[...]
    kk = kk * lax.rsqrt(jnp.maximum(_head_sums(kk * kk, hones), L2_EPS * L2_EPS))
    k = k * (1.0 + (a - 1.0) * ka_ref[...])
    b = kk * a
    aa_out[...] = (-kk * jnp.exp(cum - lw)).astype(aa_out.dtype)
    bb_out[...] = (b * e_neg).astype(bb_out.dtype)
    kb_out[...] = (k * e_neg).astype(kb_out.dtype)
    w_end = []
    for c in range(PROJ_ROWS // CHUNK):
        rows = slice(c * CHUNK, (c + 1) * CHUNK)
        cum_end = cum[(c + 1) * CHUNK - 1:(c + 1) * CHUNK, :]
        w_end.append(jnp.exp(cum_end))
        e_end = w_end[c] * e_neg[rows, :]
        be_out[rows, :] = (b[rows, :] * e_end).astype(be_out.dtype)
        ke_out[rows, :] = (k[rows, :] * e_end).astype(ke_out.dtype)
    wend_out[...] = jnp.concatenate(w_end, axis=0)

    r = shifted(0, W)
    v = shifted(2 * W, 3 * W)
    g = shifted(3 * W, 4 * W)

    C = CONV_WIDTH
    gate_b, gate_c, hc, gc = zc[:, :C], zc[:, C:2 * C], zc[:, 2 * C:3 * C], zc[:, 3 * C:]
    u = gate_c * hc
    ulast = ulast_ref[...]
    u1 = _shift_rows(u, ulast, 1)
    u2 = _shift_rows(u, ulast, 2)
    ulast_ref[...] = u[PROJ_ROWS - CARRY_ROWS:, :]
    cw = convw_ref[...]
    conv = u2 * cw[0:1, :] + u1 * cw[1:2, :] + u * cw[2:3, :]
    yconv_out[...] = (gate_b * conv * _silu(gc)).astype(yconv_out.dtype)

    ra_out[...] = (r * jnp.exp(cum)).astype(ra_out.dtype)
    if has_vres:
        mix = _sigmoid(vbias_ref[...] + _dot(_dot(v, vdown_ref[...]), vup_ref[...]))
        v = v + (vfirst_ref[...] - v) * mix
    else:
        vfirst_out[...] = v
    v_out[...] = v.astype(v_out.dtype)
    bonus_out[...] = (_head_sums(r * k * rk_ref[...], hones) * v).astype(bonus_out.dtype)
    sg_out[...] = _silu(g).astype(sg_out.dtype)


def _proj_call(x, gain, win, mu, dup, dbias, aup, abias, k_k, k_a, rk, convw, hones, vres):
    B, T, _ = x.shape
    has_vres = vres is not None
    index_map = lambda b, t: (b, t, 0)
    rows = lambda width: pl.BlockSpec((None, PROJ_ROWS, width), index_map)
    const = lambda a: pl.BlockSpec(a.shape, lambda b, t: (0,) * a.ndim)
    consts = [gain, win, mu, dup, dbias, aup, abias, k_k, k_a, rk, convw, hones]
    in_specs = [rows(D_MODEL)] + [const(a) for a in consts]
    args = [x] + consts
    bf16_out = jax.ShapeDtypeStruct((B, T, RWKV_WIDTH), BF16)
    n_chunks = PROJ_ROWS // CHUNK
    wend_spec = pl.BlockSpec((None, None, n_chunks, RWKV_WIDTH), lambda b, t: (b, t, 0, 0))
    wend_shape = jax.ShapeDtypeStruct((B, T // PROJ_ROWS, n_chunks, RWKV_WIDTH), F32)
    out_specs = [rows(RWKV_WIDTH)] * 7 + [wend_spec] + [rows(RWKV_WIDTH)] * 3
    out_shape = [bf16_out] * 7 + [wend_shape] + [bf16_out] * 3
    if has_vres:
        vfirst, vdown, vup, vbias = vres
        in_specs += [rows(RWKV_WIDTH), const(vdown), const(vup), const(vbias)]
        args += [vfirst, vdown, vup, vbias]
    else:
        out_specs = [rows(RWKV_WIDTH)] + out_specs
        out_shape = [jax.ShapeDtypeStruct((B, T, RWKV_WIDTH), F32)] + out_shape
    outs = pl.pallas_call(
        functools.partial(_proj_kernel, has_vres),
        grid=(B, T // PROJ_ROWS),
        in_specs=in_specs,
        out_specs=out_specs,
        out_shape=out_shape,
        scratch_shapes=[pltpu.VMEM((CARRY_ROWS, RWKV_COLS), F32),
                        pltpu.VMEM((CARRY_ROWS, CONV_WIDTH), F32)],
        compiler_params=pltpu.CompilerParams(
            dimension_semantics=("parallel", "arbitrary"), vmem_limit_bytes=VMEM_LIMIT),
    )(*args)
    if has_vres:
        return None, outs
    return outs[0], outs[1:]


def _wkv_kernel(final, ra_ref, aa_ref, bb_ref, kb_ref, be_ref, ke_ref, v_ref, wend_ref, bonus_ref,
                sg_ref, x_ref, yconv_ref, lng_ref, lnb_ref, gmean_ref, wout_ref, *refs):
    if final:
        fgain_ref, x_out, state_ref, gm_ref, hn_ref = refs
    else:
        x_out, state_ref, gm_ref, hn_ref = refs
    @pl.when(pl.program_id(1) == 0)
    def _():
        state_ref[...] = jnp.zeros_like(state_ref)
        gm_ref[...] = jnp.zeros_like(gm_ref)
        hn_ref[...] = jnp.zeros_like(hn_ref)

    lane = lax.broadcasted_iota(jnp.int32, (CHUNK, PAIR), 1)
    row = lax.broadcasted_iota(jnp.int32, (CHUNK, PAIR), 0)
    head0 = lane < HEAD_SIZE
    col = jnp.where(head0, lane, lane - HEAD_SIZE)
    strict = col < row
    incl = col <= row
    eye_mask = col == row

    def bd(m):
        return jnp.concatenate([jnp.where(head0, m, 0.0), jnp.where(head0, 0.0, m)], axis=0)

    gmean = gmean_ref[...]
    zeros_pp = jnp.zeros((PAIR, PAIR), F32)

    each = lambda f, *cols: [f(*xs) for xs in zip(*cols)]
    tiles = [(pl.ds(c * CHUNK, CHUNK), pl.ds(p * PAIR, PAIR))
             for c in range(WKV_CHUNKS) for p in range(N_PAIRS)]
    load = lambda ref: [ref[rw, ln].astype(F32) for rw, ln in tiles]

    state = [state_ref[p] for p in range(N_PAIRS)]
    y = []

    def recurrence(c):
        for p in range(N_PAIRS):
            i = c * N_PAIRS + p
            both = jnp.dot(gm_ref[i], bd(state[p]).astype(BF16), preferred_element_type=F32)
            hn = hn_ref[i]
            y.append(both[:CHUNK] + hn[:CHUNK])
            state[p] = both[CHUNK:] + hn[CHUNK:]

    ra, aa, bb, kb, be, ke, v = (load(ref) for ref in
                                 (ra_ref, aa_ref, bb_ref, kb_ref, be_ref, ke_ref, v_ref))
    w_end = [wend_ref[c:c + 1, pl.ds(p * PAIR, PAIR)]
             for c in range(WKV_CHUNKS) for p in range(N_PAIRS)]
    gram = each(lambda a_, r_, b_, k_: lax.dot_general(
        jnp.concatenate([a_, r_], axis=0).astype(BF16),
        jnp.concatenate([bd(b_), bd(k_)], axis=0).astype(BF16),
        (((1,), (1,)), ((), ())), preferred_element_type=F32), aa, ra, bb, kb)
    n_ab = [jnp.where(strict, g[:CHUNK, :PAIR], 0.0) for g in gram]
    a_ak = [jnp.where(strict, g[:CHUNK, PAIR:], 0.0) for g in gram]
    a_rb = [jnp.where(incl, g[CHUNK:, :PAIR], 0.0) for g in gram]
    a_rk = [jnp.where(incl, g[CHUNK:, PAIR:], 0.0) for g in gram]
    recurrence(0)

    tinv = [jnp.where(eye_mask, 1.0, n) for n in n_ab]
    power = each(lambda n: _dot(n, bd(n)), n_ab)

    def inverse_step():
        nonlocal power, tinv
        both = each(lambda p, t: _dot(jnp.concatenate([p, t], axis=0), bd(p)), power, tinv)
        power = [x[:CHUNK] for x in both]
        tinv = each(lambda t, x: t + x[CHUNK:], tinv, both)

    for c in range(1, WKV_CHUNKS):
        recurrence(c)
        inverse_step()
    for p in range(N_PAIRS):
        state_ref[p] = state[p]

    y = jnp.concatenate(y, axis=0)
    y_hi = y.astype(BF16)
    y_lo = (y - y_hi.astype(F32)).astype(BF16)
    d = y - jnp.dot(jnp.concatenate([y_hi, y_lo], axis=1), gmean, preferred_element_type=F32)
    for _ in range(WKV_CHUNKS - 1, 4):
        inverse_step()
    var = jnp.dot((d * d).astype(BF16), gmean[:PAIR], preferred_element_type=F32)
    tinv = each(lambda t, p: t + _dot(t, bd(p)), tinv, power)
    yn = d * lax.rsqrt(var + GN_EPS)
    gated = []
    for i, (rw, ln) in enumerate(tiles):
        yi = yn[i * CHUNK:(i + 1) * CHUNK] * lng_ref[:, ln] + lnb_ref[:, ln]
        gated.append(((yi + bonus_ref[rw, ln].astype(F32))
                      * sg_ref[rw, ln].astype(F32)).astype(BF16))
    y_rwkv = jnp.concatenate(
        [jnp.concatenate(gated[c * N_PAIRS:(c + 1) * N_PAIRS], axis=1) for c in range(WKV_CHUNKS)],
        axis=0)

    akv = each(lambda a_, v_: _dot(a_, bd(v_)), a_ak, v)

    x_new = (x_ref[...]
             + jnp.dot(y_rwkv, wout_ref[:RWKV_WIDTH, :], preferred_element_type=F32)
             + jnp.dot(yconv_ref[...], wout_ref[RWKV_WIDTH:, :], preferred_element_type=F32))
    if final:
        x_new = (x_new * lax.rsqrt(jnp.mean(x_new * x_new, axis=-1, keepdims=True) + NORM_EPS)
                 * fgain_ref[...])
    x_out[...] = x_new

    pq = each(lambda t, a_, q: _dot(t, jnp.concatenate([bd(a_), bd(q)], axis=1)), tinv, aa, akv)

    def fold_t(m):
        t = bd(m).T
        return t[:CHUNK] + t[CHUNK:]

    ghmn = each(lambda rb, rk, b_, k_, x, v_: _dot(
        jnp.concatenate([jnp.concatenate([rb, rk], axis=1),
                         jnp.concatenate([fold_t(b_), fold_t(k_)], axis=1)], axis=0),
        jnp.concatenate([jnp.concatenate([bd(x[:, :PAIR]), bd(x[:, PAIR:])], axis=1),
                         jnp.concatenate([zeros_pp, bd(v_)], axis=1)], axis=0)),
        a_rb, a_rk, be, ke, pq, v)
    for i in range(WKV_ENTRIES):
        gm_ref[i] = jnp.concatenate(
            [ghmn[i][:CHUNK, :PAIR] + ra[i],
             ghmn[i][CHUNK:, :PAIR] + jnp.where(eye_mask, w_end[i], 0.0)], axis=0).astype(BF16)
        hn_ref[i] = ghmn[i][:, PAIR:]


def _wkv_call(ops, wend, bonus, sg, x, yconv, lng, lnb, gmean, wout, fgain):
    B, T, _ = x.shape
    n_t = T // WKV_ROWS
    final = fgain is not None
    now = lambda bi, t: (bi, jnp.minimum(t, n_t - 1), 0)
    lagged = lambda bi, t: (bi, jnp.maximum(t - 1, 0), 0)
    blk = lambda width, index_map: pl.BlockSpec((None, WKV_ROWS, width), index_map)
    wend_spec = pl.BlockSpec((None, None, WKV_CHUNKS, RWKV_WIDTH),
                             lambda bi, t: (bi, jnp.minimum(t, n_t - 1), 0, 0))
    const = lambda a: pl.BlockSpec(a.shape, lambda bi, t: (0, 0))
    in_specs = ([blk(RWKV_WIDTH, now)] * 7
                + [wend_spec, blk(RWKV_WIDTH, lagged), blk(RWKV_WIDTH, lagged),
                   blk(D_MODEL, lagged), blk(CONV_WIDTH, lagged),
                   const(lng), const(lnb), const(gmean), const(wout)])
    args = [*ops, wend, bonus, sg, x, yconv, lng, lnb, gmean, wout]
    if final:
        in_specs.append(const(fgain))
        args.append(fgain)
    return pl.pallas_call(
        functools.partial(_wkv_kernel, final),
        grid=(B, n_t + 1),
        in_specs=in_specs,
        out_specs=blk(D_MODEL, lagged),
        out_shape=jax.ShapeDtypeStruct((B, T, D_MODEL), F32),
        scratch_shapes=[pltpu.VMEM((N_PAIRS, CHUNK, PAIR), F32),
                        pltpu.VMEM((WKV_ENTRIES, 2 * CHUNK, PAIR), BF16),
                        pltpu.VMEM((WKV_ENTRIES, 2 * CHUNK, PAIR), F32)],
        compiler_params=pltpu.CompilerParams(
            dimension_semantics=("parallel", "arbitrary"), vmem_limit_bytes=VMEM_LIMIT),
    )(*args)


def kernel(x, norm_gain, w_in, shift_mu, decay_up, decay_bias, aaa_up, aaa_bias, k_k, k_a, r_k,
           ln_gain, ln_bias, v_down, v_up, v_bias, conv_w, w_out, final_gain):
    depth = w_in.shape[0]
    B, T, _ = x.shape
    assert T % WKV_ROWS == 0 and T % PROJ_ROWS == 0
    assert PROJ_ROWS % CHUNK == 0 and HEAD_SIZE == CHUNK
    assert 1 <= WKV_CHUNKS <= 5

    idx = jnp.arange(MXU_DIM)
    hones = (idx[:, None] // HEAD_SIZE == idx[None, :] // HEAD_SIZE).astype(BF16)
    pidx = jnp.arange(PAIR)
    gmean = ((pidx[:, None] // HEAD_SIZE == pidx[None, :] // HEAD_SIZE).astype(F32)
             / HEAD_SIZE).astype(BF16)
    gmean = jnp.concatenate([gmean, gmean], axis=0)
    lora_pad = jnp.zeros((D_LORA, RWKV_WIDTH), F32)

    row = lambda a: a.reshape(1, -1)
    v_first = None
    for l in range(depth):
        dup = jnp.concatenate([decay_up[l], lora_pad], axis=0).astype(BF16)
        aup = jnp.concatenate([lora_pad, aaa_up[l]], axis=0).astype(BF16)
        vres = None
        if l > 0:
            vres = (v_first, v_down[l - 1].astype(BF16), v_up[l - 1].astype(BF16), row(v_bias[l - 1]))
        new_v_first, outs = _proj_call(
            x, row(norm_gain[l]), w_in[l].astype(BF16), row(shift_mu[l]), dup, row(decay_bias[l]),
            aup, row(aaa_bias[l]), row(k_k[l]), row(k_a[l]), row(r_k[l]), conv_w[l], hones, vres)
        if l == 0:
            v_first = new_v_first
        ops, wend, bonus, sg, yconv = outs[:7], outs[7], outs[8], outs[9], outs[10]
        wend = wend.reshape(B, T // WKV_ROWS, WKV_CHUNKS, RWKV_WIDTH)
        fgain = row(final_gain) if l == depth - 1 else None
        x = _wkv_call(ops, wend, bonus, sg, x, yconv, row(ln_gain[l]), row(ln_bias[l]), gmean,
                      w_out[l].astype(BF16), fgain)
    return x
```

```python
import functools

import jax
import jax.numpy as jnp
from jax import lax
from jax.experimental import pallas as pl
from jax.experimental.pallas import tpu as pltpu

D_MODEL = 1024
RWKV_WIDTH = 512
CONV_WIDTH = 512
HEAD_SIZE = 64
CONV_K = 3
D_LORA = 64
RWKV_COLS = 4 * RWKV_WIDTH + 2 * D_LORA
CONV_COLS = 4 * CONV_WIDTH
NORM_EPS = 1e-6
GN_EPS = 1e-5 * HEAD_SIZE
L2_EPS = 1e-12

MXU_DIM = 256
CHUNK = 64
PAIR = 2 * HEAD_SIZE
N_PAIRS = RWKV_WIDTH // PAIR
N_OPS = 7
N_AUX = 3
PROJ_ROWS = 512
WKV_ROWS = 256
WKV_CHUNKS = WKV_ROWS // CHUNK
WKV_ENTRIES = WKV_CHUNKS * N_PAIRS
CARRY_ROWS = 8
VMEM_LIMIT = 56 * 1024 * 1024

F32 = jnp.float32
BF16 = jnp.bfloat16


def _dot(a, b):
    return jnp.dot(a.astype(BF16), b.astype(BF16), preferred_element_type=F32)


def _head_sums(x, hones):
    xb = x.astype(BF16)
    return jnp.concatenate(
        [jnp.dot(xb[:, i:i + MXU_DIM], hones, preferred_element_type=F32)
         for i in range(0, x.shape[1], MXU_DIM)], axis=1)


def _shift_rows(x, carry, n):
    rolled = pltpu.roll(x, n, 0)
    rows = lax.broadcasted_iota(jnp.int32, (CARRY_ROWS, x.shape[1]), 0)
    head = rolled[:CARRY_ROWS]
    for j in range(n):
        head = jnp.where(rows == j, carry[CARRY_ROWS - n + j:CARRY_ROWS - n + j + 1, :], head)
    return jnp.concatenate([head, rolled[CARRY_ROWS:]], axis=0)


def _chunk_running_sum(x):
    rows = lax.broadcasted_iota(jnp.int32, x.shape, 0) % CHUNK
    shift = 1
    while shift < CARRY_ROWS:
        x = x + jnp.where(rows >= shift, pltpu.roll(x, shift, 0), 0.0)
        shift *= 2
    while shift < CHUNK:
        parts = []
        for lo in range(0, x.shape[0], CHUNK):
            parts += [x[lo:lo + shift], x[lo + shift:lo + CHUNK] + x[lo:lo + CHUNK - shift]]
        x = jnp.concatenate(parts, axis=0)
        shift *= 2
    return x


def _sigmoid(x):
    return 0.5 * jnp.tanh(0.5 * x) + 0.5


def _silu(x):
    half = 0.5 * x
    return half * jnp.tanh(half) + half


def _proj_kernel(has_vres, *refs):
    (x_ref, gain_ref, win_ref, mu_ref, dup_ref, dbias_ref, aup_ref, abias_ref, kk_ref, ka_ref,
     rk_ref, convw_ref, hones_ref) = refs[:13]
    pos = 13
    if has_vres:
        vfirst_ref, vdown_ref, vup_ref, vbias_ref = refs[pos:pos + 4]
        pos += 4
    else:
        vfirst_out = refs[pos]
        pos += 1
    ops_out, wend_out, aux_out, zlast_ref, ulast_ref = refs[pos:]
    W = RWKV_WIDTH
    ra_out, aa_out, bb_out, kb_out, be_out, ke_out, v_out = (
        ops_out.at[:, i * W:(i + 1) * W] for i in range(N_OPS))
    bonus_out, sg_out, yconv_out = (aux_out.at[:, i * W:(i + 1) * W] for i in range(N_AUX))

    @pl.when(pl.program_id(1) == 0)
    def _():
        zlast_ref[...] = jnp.zeros_like(zlast_ref)
        ulast_ref[...] = jnp.zeros_like(ulast_ref)

    x = x_ref[...]
    h = x * lax.rsqrt(jnp.mean(x * x, axis=-1, keepdims=True) + NORM_EPS) * gain_ref[...]
    hb = h.astype(BF16)

    def shifted(lo, hi):
        z = jnp.dot(hb, win_ref[:, lo:hi], preferred_element_type=F32)
        prev = _shift_rows(z, zlast_ref[:, lo:hi], 1)
        zlast_ref[:, lo:hi] = z[PROJ_ROWS - CARRY_ROWS:, :]
        return z + (prev - z) * mu_ref[:, lo:hi]

    zwa = shifted(4 * W, RWKV_COLS)
    k = shifted(W, 2 * W)

    lw = -jnp.exp(-0.5) * _sigmoid(dbias_ref[...] + _dot(jnp.tanh(zwa), dup_ref[...]))
    a = _sigmoid(abias_ref[...] + _dot(zwa, aup_ref[...]))
    zc = jnp.dot(hb, win_ref[:, RWKV_COLS:], preferred_element_type=F32)
    cum = _chunk_running_sum(lw)
    e_neg = jnp.exp(-cum)

    hones = hones_ref[...]
    kk = k * kk_ref[...]
    kk = kk * lax.rsqrt(jnp.maximum(_head_sums(kk * kk, hones), L2_EPS * L2_EPS))
    k = k * (1.0 + (a - 1.0) * ka_ref[...])
    b = kk * a
    aa_out[...] = (-kk * jnp.exp(cum - lw)).astype(aa_out.dtype)
    bb_out[...] = (b * e_neg).astype(bb_out.dtype)
    kb_out[...] = (k * e_neg).astype(kb_out.dtype)
    w_end = []
    for c in range(PROJ_ROWS // CHUNK):
        rows = slice(c * CHUNK, (c + 1) * CHUNK)
        cum_end = cum[(c + 1) * CHUNK - 1:(c + 1) * CHUNK, :]
        w_end.append(jnp.exp(cum_end))
        e_end = w_end[c] * e_neg[rows, :]
        be_out[rows, :] = (b[rows, :] * e_end).astype(be_out.dtype)
        ke_out[rows, :] = (k[rows, :] * e_end).astype(ke_out.dtype)
    wend_out[...] = jnp.concatenate(w_end, axis=0)

    r = shifted(0, W)
    v = shifted(2 * W, 3 * W)
    g = shifted(3 * W, 4 * W)

    C = CONV_WIDTH
    gate_b, gate_c, hc, gc = zc[:, :C], zc[:, C:2 * C], zc[:, 2 * C:3 * C], zc[:, 3 * C:]
    u = gate_c * hc
    ulast = ulast_ref[...]
    u1 = _shift_rows(u, ulast, 1)
    u2 = _shift_rows(u, ulast, 2)
    ulast_ref[...] = u[PROJ_ROWS - CARRY_ROWS:, :]
    cw = convw_ref[...]
    conv = u2 * cw[0:1, :] + u1 * cw[1:2, :] + u * cw[2:3, :]
    yconv_out[...] = (gate_b * conv * _silu(gc)).astype(yconv_out.dtype)

    ra_out[...] = (r * jnp.exp(cum)).astype(ra_out.dtype)
    if has_vres:
        mix = _sigmoid(vbias_ref[...] + _dot(_dot(v, vdown_ref[...]), vup_ref[...]))
        v = v + (vfirst_ref[...] - v) * mix
    else:
        vfirst_out[...] = v
    v_out[...] = v.astype(v_out.dtype)
    bonus_out[...] = (_head_sums(r * k * rk_ref[...], hones) * v).astype(bonus_out.dtype)
    sg_out[...] = _silu(g).astype(sg_out.dtype)


def _proj_call(x, gain, win, mu, dup, dbias, aup, abias, k_k, k_a, rk, convw, hones, vres):
    B, T, _ = x.shape
    has_vres = vres is not None
    index_map = lambda b, t: (b, t, 0)
    rows = lambda width: pl.BlockSpec((None, PROJ_ROWS, width), index_map)
    const = lambda a: pl.BlockSpec(a.shape, lambda b, t: (0,) * a.ndim)
    consts = [gain, win, mu, dup, dbias, aup, abias, k_k, k_a, rk, convw, hones]
    in_specs = [rows(D_MODEL)] + [const(a) for a in consts]
    args = [x] + consts
    bf16_out = lambda n: jax.ShapeDtypeStruct((B, T, n * RWKV_WIDTH), BF16)
    n_chunks = PROJ_ROWS // CHUNK
    wend_spec = pl.BlockSpec((None, None, n_chunks, RWKV_WIDTH), lambda b, t: (b, t, 0, 0))
    wend_shape = jax.ShapeDtypeStruct((B, T // PROJ_ROWS, n_chunks, RWKV_WIDTH), F32)
    out_specs = [rows(N_OPS * RWKV_WIDTH), wend_spec, rows(N_AUX * RWKV_WIDTH)]
    out_shape = [bf16_out(N_OPS), wend_shape, bf16_out(N_AUX)]
    if has_vres:
        vfirst, vdown, vup, vbias = vres
        in_specs += [rows(RWKV_WIDTH), const(vdown), const(vup), const(vbias)]
        args += [vfirst, vdown, vup, vbias]
    else:
        out_specs = [rows(RWKV_WIDTH)] + out_specs
        out_shape = [jax.ShapeDtypeStruct((B, T, RWKV_WIDTH), F32)] + out_shape
    outs = pl.pallas_call(
        functools.partial(_proj_kernel, has_vres),
        grid=(B, T // PROJ_ROWS),
        in_specs=in_specs,
        out_specs=out_specs,
        out_shape=out_shape,
        scratch_shapes=[pltpu.VMEM((CARRY_ROWS, RWKV_COLS), F32),
                        pltpu.VMEM((CARRY_ROWS, CONV_WIDTH), F32)],
        compiler_params=pltpu.CompilerParams(
            dimension_semantics=("parallel", "arbitrary"), vmem_limit_bytes=VMEM_LIMIT),
    )(*args)
    if has_vres:
        return None, outs
    return outs[0], outs[1:]


def _wkv_kernel(final, ops_ref, wend_ref, aux_ref, x_ref, lng_ref, lnb_ref, gmean_ref, wout_ref,
                *refs):
    if final:
        fgain_ref, x_out, state_ref, gm_ref, hn_ref = refs
    else:
        x_out, state_ref, gm_ref, hn_ref = refs
    W = RWKV_WIDTH
    ra_ref, aa_ref, bb_ref, kb_ref, be_ref, ke_ref, v_ref = (
        ops_ref.at[:, i * W:(i + 1) * W] for i in range(N_OPS))
    bonus_ref, sg_ref, yconv_ref = (aux_ref.at[:, i * W:(i + 1) * W] for i in range(N_AUX))
    @pl.when(pl.program_id(1) == 0)
    def _():
        state_ref[...] = jnp.zeros_like(state_ref)
        gm_ref[...] = jnp.zeros_like(gm_ref)
        hn_ref[...] = jnp.zeros_like(hn_ref)

    lane = lax.broadcasted_iota(jnp.int32, (CHUNK, PAIR), 1)
    row = lax.broadcasted_iota(jnp.int32, (CHUNK, PAIR), 0)
    head0 = lane < HEAD_SIZE
    col = jnp.where(head0, lane, lane - HEAD_SIZE)
    strict = col < row
    incl = col <= row
    eye_mask = col == row

    def bd(m):
        return jnp.concatenate([jnp.where(head0, m, 0.0), jnp.where(head0, 0.0, m)], axis=0)

    gmean = gmean_ref[...]
    zeros_pp = jnp.zeros((PAIR, PAIR), F32)

    each = lambda f, *cols: [f(*xs) for xs in zip(*cols)]
    tiles = [(pl.ds(c * CHUNK, CHUNK), pl.ds(p * PAIR, PAIR))
             for c in range(WKV_CHUNKS) for p in range(N_PAIRS)]
    load = lambda ref: [ref[rw, ln].astype(F32) for rw, ln in tiles]

    state = [state_ref[p] for p in range(N_PAIRS)]
    y = []

    def recurrence(c):
        for p in range(N_PAIRS):
            i = c * N_PAIRS + p
            both = jnp.dot(gm_ref[i], bd(state[p]).astype(BF16), preferred_element_type=F32)
            hn = hn_ref[i]
            y.append(both[:CHUNK] + hn[:CHUNK])
            state[p] = both[CHUNK:] + hn[CHUNK:]

    ra, aa, bb, kb, be, ke, v = (load(ref) for ref in
                                 (ra_ref, aa_ref, bb_ref, kb_ref, be_ref, ke_ref, v_ref))
    w_end = [wend_ref[c:c + 1, pl.ds(p * PAIR, PAIR)]
             for c in range(WKV_CHUNKS) for p in range(N_PAIRS)]
    gram = each(lambda a_, r_, b_, k_: lax.dot_general(
        jnp.concatenate([a_, r_], axis=0).astype(BF16),
        jnp.concatenate([bd(b_), bd(k_)], axis=0).astype(BF16),
        (((1,), (1,)), ((), ())), preferred_element_type=F32), aa, ra, bb, kb)
    n_ab = [jnp.where(strict, g[:CHUNK, :PAIR], 0.0) for g in gram]
    a_ak = [jnp.where(strict, g[:CHUNK, PAIR:], 0.0) for g in gram]
    a_rb = [jnp.where(incl, g[CHUNK:, :PAIR], 0.0) for g in gram]
    a_rk = [jnp.where(incl, g[CHUNK:, PAIR:], 0.0) for g in gram]
    recurrence(0)

    tinv = [jnp.where(eye_mask, 1.0, n) for n in n_ab]
    power = each(lambda n: _dot(n, bd(n)), n_ab)

    def inverse_step():
        nonlocal power, tinv
        both = each(lambda p, t: _dot(jnp.concatenate([p, t], axis=0), bd(p)), power, tinv)
        power = [x[:CHUNK] for x in both]
        tinv = each(lambda t, x: t + x[CHUNK:], tinv, both)

    for c in range(1, WKV_CHUNKS):
        recurrence(c)
        inverse_step()
    for p in range(N_PAIRS):
        state_ref[p] = state[p]

    y = jnp.concatenate(y, axis=0)
    y_hi = y.astype(BF16)
    y_lo = (y - y_hi.astype(F32)).astype(BF16)
    d = y - jnp.dot(jnp.concatenate([y_hi, y_lo], axis=1), gmean, preferred_element_type=F32)
    for _ in range(WKV_CHUNKS - 1, 4):
        inverse_step()
    var = jnp.dot((d * d).astype(BF16), gmean[:PAIR], preferred_element_type=F32)
    tinv = each(lambda t, p: t + _dot(t, bd(p)), tinv, power)
    yn = d * lax.rsqrt(var + GN_EPS)
    gated = []
    for i, (rw, ln) in enumerate(tiles):
        yi = yn[i * CHUNK:(i + 1) * CHUNK] * lng_ref[:, ln] + lnb_ref[:, ln]
        gated.append(((yi + bonus_ref[rw, ln].astype(F32))
                      * sg_ref[rw, ln].astype(F32)).astype(BF16))
    y_rwkv = jnp.concatenate(
        [jnp.concatenate(gated[c * N_PAIRS:(c + 1) * N_PAIRS], axis=1) for c in range(WKV_CHUNKS)],
        axis=0)

    akv = each(lambda a_, v_: _dot(a_, bd(v_)), a_ak, v)

    x_new = (x_ref[...]
             + jnp.dot(y_rwkv, wout_ref[:RWKV_WIDTH, :], preferred_element_type=F32)
             + jnp.dot(yconv_ref[...], wout_ref[RWKV_WIDTH:, :], preferred_element_type=F32))
    if final:
        x_new = (x_new * lax.rsqrt(jnp.mean(x_new * x_new, axis=-1, keepdims=True) + NORM_EPS)
                 * fgain_ref[...])
    x_out[...] = x_new

    pq = each(lambda t, a_, q: _dot(t, jnp.concatenate([bd(a_), bd(q)], axis=1)), tinv, aa, akv)

    def fold_t(m):
        t = bd(m).T
        return t[:CHUNK] + t[CHUNK:]

    ghmn = each(lambda rb, rk, b_, k_, x, v_: _dot(
        jnp.concatenate([jnp.concatenate([rb, rk], axis=1),
                         jnp.concatenate([fold_t(b_), fold_t(k_)], axis=1)], axis=0),
        jnp.concatenate([jnp.concatenate([bd(x[:, :PAIR]), bd(x[:, PAIR:])], axis=1),
                         jnp.concatenate([zeros_pp, bd(v_)], axis=1)], axis=0)),
        a_rb, a_rk, be, ke, pq, v)
    for i in range(WKV_ENTRIES):
        gm_ref[i] = jnp.concatenate(
            [ghmn[i][:CHUNK, :PAIR] + ra[i],
             ghmn[i][CHUNK:, :PAIR] + jnp.where(eye_mask, w_end[i], 0.0)], axis=0).astype(BF16)
        hn_ref[i] = ghmn[i][:, PAIR:]


def _wkv_call(ops, wend, aux, x, lng, lnb, gmean, wout, fgain):
    B, T, _ = x.shape
    n_t = T // WKV_ROWS
    final = fgain is not None
    now = lambda bi, t: (bi, jnp.minimum(t, n_t - 1), 0)
    lagged = lambda bi, t: (bi, jnp.maximum(t - 1, 0), 0)
    blk = lambda width, index_map: pl.BlockSpec((None, WKV_ROWS, width), index_map)
    wend_spec = pl.BlockSpec((None, None, WKV_CHUNKS, RWKV_WIDTH),
                             lambda bi, t: (bi, jnp.minimum(t, n_t - 1), 0, 0))
    const = lambda a: pl.BlockSpec(a.shape, lambda bi, t: (0, 0))
    in_specs = [blk(N_OPS * RWKV_WIDTH, now), wend_spec, blk(N_AUX * RWKV_WIDTH, lagged),
                blk(D_MODEL, lagged), const(lng), const(lnb), const(gmean), const(wout)]
    args = [ops, wend, aux, x, lng, lnb, gmean, wout]
    if final:
        in_specs.append(const(fgain))
        args.append(fgain)
    return pl.pallas_call(
        functools.partial(_wkv_kernel, final),
        grid=(B, n_t + 1),
        in_specs=in_specs,
        out_specs=blk(D_MODEL, lagged),
        out_shape=jax.ShapeDtypeStruct((B, T, D_MODEL), F32),
        scratch_shapes=[pltpu.VMEM((N_PAIRS, CHUNK, PAIR), F32),
                        pltpu.VMEM((WKV_ENTRIES, 2 * CHUNK, PAIR), BF16),
                        pltpu.VMEM((WKV_ENTRIES, 2 * CHUNK, PAIR), F32)],
        compiler_params=pltpu.CompilerParams(
            dimension_semantics=("parallel", "arbitrary"), vmem_limit_bytes=VMEM_LIMIT),
    )(*args)


def kernel(x, norm_gain, w_in, shift_mu, decay_up, decay_bias, aaa_up, aaa_bias, k_k, k_a, r_k,
           ln_gain, ln_bias, v_down, v_up, v_bias, conv_w, w_out, final_gain):
    depth = w_in.shape[0]
    B, T, _ = x.shape
    assert T % WKV_ROWS == 0 and T % PROJ_ROWS == 0
    assert PROJ_ROWS % CHUNK == 0 and HEAD_SIZE == CHUNK
    assert 1 <= WKV_CHUNKS <= 5

    idx = jnp.arange(MXU_DIM)
    hones = (idx[:, None] // HEAD_SIZE == idx[None, :] // HEAD_SIZE).astype(BF16)
    pidx = jnp.arange(PAIR)
    gmean = ((pidx[:, None] // HEAD_SIZE == pidx[None, :] // HEAD_SIZE).astype(F32)
             / HEAD_SIZE).astype(BF16)
    gmean = jnp.concatenate([gmean, gmean], axis=0)
    lora_pad = jnp.zeros((D_LORA, RWKV_WIDTH), F32)

    row = lambda a: a.reshape(1, -1)
    v_first = None
    for l in range(depth):
        dup = jnp.concatenate([decay_up[l], lora_pad], axis=0).astype(BF16)
        aup = jnp.concatenate([lora_pad, aaa_up[l]], axis=0).astype(BF16)
        vres = None
        if l > 0:
            vres = (v_first, v_down[l - 1].astype(BF16), v_up[l - 1].astype(BF16), row(v_bias[l - 1]))
        new_v_first, outs = _proj_call(
            x, row(norm_gain[l]), w_in[l].astype(BF16), row(shift_mu[l]), dup, row(decay_bias[l]),
            aup, row(aaa_bias[l]), row(k_k[l]), row(k_a[l]), row(r_k[l]), conv_w[l], hones, vres)
        if l == 0:
            v_first = new_v_first
        ops, wend, aux = outs
        wend = wend.reshape(B, T // WKV_ROWS, WKV_CHUNKS, RWKV_WIDTH)
        fgain = row(final_gain) if l == depth - 1 else None
        x = _wkv_call(ops, wend, aux, x, row(ln_gain[l]), row(ln_bias[l]), gmean,
                      w_out[l].astype(BF16), fgain)
    return x
```

```python
import functools

import jax
import jax.numpy as jnp
from jax import lax
from jax.experimental import pallas as pl
from jax.experimental.pallas import tpu as pltpu

D_MODEL = 1024
RWKV_WIDTH = 512
CONV_WIDTH = 512
HEAD_SIZE = 64
CONV_K = 3
D_LORA = 64
RWKV_COLS = 4 * RWKV_WIDTH + 2 * D_LORA
CONV_COLS = 4 * CONV_WIDTH
NORM_EPS = 1e-6
GN_EPS = 1e-5 * HEAD_SIZE
L2_EPS = 1e-12

MXU_DIM = 256
CHUNK = 64
PAIR = 2 * HEAD_SIZE
N_PAIRS = RWKV_WIDTH // PAIR
N_OPS = 7
N_AUX = 3
PROJ_ROWS = 512
WKV_ROWS = 256
WKV_CHUNKS = WKV_ROWS // CHUNK
WKV_ENTRIES = WKV_CHUNKS * N_PAIRS
CARRY_ROWS = 8
VMEM_LIMIT = 56 * 1024 * 1024

F32 = jnp.float32
BF16 = jnp.bfloat16


def _dot(a, b):
    return jnp.dot(a.astype(BF16), b.astype(BF16), preferred_element_type=F32)


def _head_sums(x, hones):
    xb = x.astype(BF16)
    return jnp.concatenate(
        [jnp.dot(xb[:, i:i + MXU_DIM], hones, preferred_element_type=F32)
         for i in range(0, x.shape[1], MXU_DIM)], axis=1)


def _shift_rows(x, carry, n):
    rolled = pltpu.roll(x, n, 0)
    rows = lax.broadcasted_iota(jnp.int32, (CARRY_ROWS, x.shape[1]), 0)
    head = rolled[:CARRY_ROWS]
    for j in range(n):
        head = jnp.where(rows == j, carry[CARRY_ROWS - n + j:CARRY_ROWS - n + j + 1, :], head)
    return jnp.concatenate([head, rolled[CARRY_ROWS:]], axis=0)


def _chunk_running_sum(x):
    rows = lax.broadcasted_iota(jnp.int32, x.shape, 0) % CHUNK
    shift = 1
    while shift < CARRY_ROWS:
        x = x + jnp.where(rows >= shift, pltpu.roll(x, shift, 0), 0.0)
        shift *= 2
    while shift < CHUNK:
        parts = []
        for lo in range(0, x.shape[0], CHUNK):
            parts += [x[lo:lo + shift], x[lo + shift:lo + CHUNK] + x[lo:lo + CHUNK - shift]]
        x = jnp.concatenate(parts, axis=0)
        shift *= 2
    return x


def _sigmoid(x):
    return 0.5 * jnp.tanh(0.5 * x) + 0.5


def _silu(x):
    half = 0.5 * x
    return half * jnp.tanh(half) + half


def _proj_kernel(has_vres, *refs):
    (x_ref, gain_ref, win_ref, mu_ref, dup_ref, dbias_ref, aup_ref, abias_ref, kk_ref, ka_ref,
     rk_ref, convw_ref, hones_ref) = refs[:13]
    pos = 13
    if has_vres:
        vfirst_ref, vdown_ref, vup_ref, vbias_ref = refs[pos:pos + 4]
        pos += 4
    else:
        vfirst_out = refs[pos]
        pos += 1
    ops_out, wend_out, aux_out, zlast_ref, ulast_ref = refs[pos:]
    W = RWKV_WIDTH
    ra_out, aa_out, bb_out, kb_out, be_out, ke_out, v_out = (
        ops_out.at[:, i * W:(i + 1) * W] for i in range(N_OPS))
    bonus_out, sg_out, yconv_out = (aux_out.at[:, i * W:(i + 1) * W] for i in range(N_AUX))

    @pl.when(pl.program_id(1) == 0)
    def _():
        zlast_ref[...] = jnp.zeros_like(zlast_ref)
        ulast_ref[...] = jnp.zeros_like(ulast_ref)

    x = x_ref[...]
    h = x * lax.rsqrt(jnp.mean(x * x, axis=-1, keepdims=True) + NORM_EPS) * gain_ref[...]
    hb = h.astype(BF16)

    def shifted(lo, hi):
        z = jnp.dot(hb, win_ref[:, lo:hi], preferred_element_type=F32)
        prev = _shift_rows(z, zlast_ref[:, lo:hi], 1)
        zlast_ref[:, lo:hi] = z[PROJ_ROWS - CARRY_ROWS:, :]
        return z + (prev - z) * mu_ref[:, lo:hi]

    zwa = shifted(4 * W, RWKV_COLS)
    k = shifted(W, 2 * W)

    lw = -jnp.exp(-0.5) * _sigmoid(dbias_ref[...] + _dot(jnp.tanh(zwa), dup_ref[...]))
    a = _sigmoid(abias_ref[...] + _dot(zwa, aup_ref[...]))
    zc = jnp.dot(hb, win_ref[:, RWKV_COLS:], preferred_element_type=F32)
    cum = _chunk_running_sum(lw)
    e_neg = jnp.exp(-cum)

    hones = hones_ref[...]
    kk = k * kk_ref[...]
    kk = kk * lax.rsqrt(jnp.maximum(_head_sums(kk * kk, hones), L2_EPS * L2_EPS))
    k = k * (1.0 + (a - 1.0) * ka_ref[...])
    b = kk * a
    aa_out[...] = (-kk * jnp.exp(cum - lw)).astype(aa_out.dtype)
    bb_out[...] = (b * e_neg).astype(bb_out.dtype)
    kb_out[...] = (k * e_neg).astype(kb_out.dtype)
    w_end = []
    for c in range(PROJ_ROWS // CHUNK):
        rows = slice(c * CHUNK, (c + 1) * CHUNK)
        cum_end = cum[(c + 1) * CHUNK - 1:(c + 1) * CHUNK, :]
        w_end.append(jnp.exp(cum_end))
        e_end = w_end[c] * e_neg[rows, :]
        be_out[rows, :] = (b[rows, :] * e_end).astype(be_out.dtype)
        ke_out[rows, :] = (k[rows, :] * e_end).astype(ke_out.dtype)
    wend_out[...] = jnp.concatenate(w_end, axis=0)

    r = shifted(0, W)
    v = shifted(2 * W, 3 * W)
    g = shifted(3 * W, 4 * W)

    C = CONV_WIDTH
    gate_b, gate_c, hc, gc = zc[:, :C], zc[:, C:2 * C], zc[:, 2 * C:3 * C], zc[:, 3 * C:]
    u = gate_c * hc
    ulast = ulast_ref[...]
    u1 = _shift_rows(u, ulast, 1)
    u2 = _shift_rows(u, ulast, 2)
    ulast_ref[...] = u[PROJ_ROWS - CARRY_ROWS:, :]
    cw = convw_ref[...]
    conv = u2 * cw[0:1, :] + u1 * cw[1:2, :] + u * cw[2:3, :]
    yconv_out[...] = (gate_b * conv * _silu(gc)).astype(yconv_out.dtype)

    ra_out[...] = (r * jnp.exp(cum)).astype(ra_out.dtype)
    if has_vres:
        mix = _sigmoid(vbias_ref[...] + _dot(_dot(v, vdown_ref[...]), vup_ref[...]))
        v = v + (vfirst_ref[...] - v) * mix
    else:
        vfirst_out[...] = v
    v_out[...] = v.astype(v_out.dtype)
    bonus_out[...] = (_head_sums(r * k * rk_ref[...], hones) * v).astype(bonus_out.dtype)
    sg_out[...] = _silu(g).astype(sg_out.dtype)


def _proj_call(x, gain, win, mu, dup, dbias, aup, abias, k_k, k_a, rk, convw, hones, vres):
    B, T, _ = x.shape
    has_vres = vres is not None
    index_map = lambda b, t: (b, t, 0)
    rows = lambda width: pl.BlockSpec((None, PROJ_ROWS, width), index_map)
    const = lambda a: pl.BlockSpec(a.shape, lambda b, t: (0,) * a.ndim)
    consts = [gain, win, mu, dup, dbias, aup, abias, k_k, k_a, rk, convw, hones]
    in_specs = [rows(D_MODEL)] + [const(a) for a in consts]
    args = [x] + consts
    bf16_out = lambda n: jax.ShapeDtypeStruct((B, T, n * RWKV_WIDTH), BF16)
    n_chunks = PROJ_ROWS // CHUNK
    wend_spec = pl.BlockSpec((None, None, n_chunks, RWKV_WIDTH), lambda b, t: (b, t, 0, 0))
    wend_shape = jax.ShapeDtypeStruct((B, T // PROJ_ROWS, n_chunks, RWKV_WIDTH), F32)
    out_specs = [rows(N_OPS * RWKV_WIDTH), wend_spec, rows(N_AUX * RWKV_WIDTH)]
    out_shape = [bf16_out(N_OPS), wend_shape, bf16_out(N_AUX)]
    if has_vres:
        vfirst, vdown, vup, vbias = vres
        in_specs += [rows(RWKV_WIDTH), const(vdown), const(vup), const(vbias)]
        args += [vfirst, vdown, vup, vbias]
    else:
        out_specs = [rows(RWKV_WIDTH)] + out_specs
        out_shape = [jax.ShapeDtypeStruct((B, T, RWKV_WIDTH), F32)] + out_shape
    outs = pl.pallas_call(
        functools.partial(_proj_kernel, has_vres),
        grid=(B, T // PROJ_ROWS),
        in_specs=in_specs,
        out_specs=out_specs,
        out_shape=out_shape,
        scratch_shapes=[pltpu.VMEM((CARRY_ROWS, RWKV_COLS), F32),
                        pltpu.VMEM((CARRY_ROWS, CONV_WIDTH), F32)],
        compiler_params=pltpu.CompilerParams(
            dimension_semantics=("parallel", "arbitrary"), vmem_limit_bytes=VMEM_LIMIT),
    )(*args)
    if has_vres:
        return None, outs
    return outs[0], outs[1:]


def _wkv_kernel(final, blocks_per_row, ops_ref, wend_ref, aux_ref, x_ref, lng_ref, lnb_ref,
                gmean_ref, wout_ref, *refs):
    if final:
        fgain_ref, x_out, state_ref, gm_ref, hn_ref = refs
    else:
        x_out, state_ref, gm_ref, hn_ref = refs
    W = RWKV_WIDTH
    ra_ref, aa_ref, bb_ref, kb_ref, be_ref, ke_ref, v_ref = (
        ops_ref.at[:, i * W:(i + 1) * W] for i in range(N_OPS))
    bonus_ref, sg_ref, yconv_ref = (aux_ref.at[:, i * W:(i + 1) * W] for i in range(N_AUX))
    step = pl.program_id(0)

    @pl.when(step == 0)
    def _():
        gm_ref[...] = jnp.zeros_like(gm_ref)
        hn_ref[...] = jnp.zeros_like(hn_ref)

    @pl.when(jnp.maximum(step - 1, 0) % blocks_per_row == 0)
    def _():
        state_ref[...] = jnp.zeros_like(state_ref)

    lane = lax.broadcasted_iota(jnp.int32, (CHUNK, PAIR), 1)
    row = lax.broadcasted_iota(jnp.int32, (CHUNK, PAIR), 0)
    head0 = lane < HEAD_SIZE
    col = jnp.where(head0, lane, lane - HEAD_SIZE)
    strict = col < row
    incl = col <= row
    eye_mask = col == row

    def bd(m):
        return jnp.concatenate([jnp.where(head0, m, 0.0), jnp.where(head0, 0.0, m)], axis=0)

    gmean = gmean_ref[...]
    zeros_pp = jnp.zeros((PAIR, PAIR), F32)

    each = lambda f, *cols: [f(*xs) for xs in zip(*cols)]
    tiles = [(pl.ds(c * CHUNK, CHUNK), pl.ds(p * PAIR, PAIR))
             for c in range(WKV_CHUNKS) for p in range(N_PAIRS)]
    load = lambda ref: [ref[rw, ln].astype(F32) for rw, ln in tiles]

    state = [state_ref[p] for p in range(N_PAIRS)]
    y = []

    def recurrence(c):
        for p in range(N_PAIRS):
            i = c * N_PAIRS + p
            both = jnp.dot(gm_ref[i], bd(state[p]).astype(BF16), preferred_element_type=F32)
            hn = hn_ref[i]
            y.append(both[:CHUNK] + hn[:CHUNK])
            state[p] = both[CHUNK:] + hn[CHUNK:]

    ra, aa, bb, kb, be, ke, v = (load(ref) for ref in
                                 (ra_ref, aa_ref, bb_ref, kb_ref, be_ref, ke_ref, v_ref))
    w_end = [wend_ref[c:c + 1, pl.ds(p * PAIR, PAIR)]
             for c in range(WKV_CHUNKS) for p in range(N_PAIRS)]
    gram = each(lambda a_, r_, b_, k_: lax.dot_general(
        jnp.concatenate([a_, r_], axis=0).astype(BF16),
        jnp.concatenate([bd(b_), bd(k_)], axis=0).astype(BF16),
        (((1,), (1,)), ((), ())), preferred_element_type=F32), aa, ra, bb, kb)
    n_ab = [jnp.where(strict, g[:CHUNK, :PAIR], 0.0) for g in gram]
    a_ak = [jnp.where(strict, g[:CHUNK, PAIR:], 0.0) for g in gram]
    a_rb = [jnp.where(incl, g[CHUNK:, :PAIR], 0.0) for g in gram]
    a_rk = [jnp.where(incl, g[CHUNK:, PAIR:], 0.0) for g in gram]
    recurrence(0)

    tinv = [jnp.where(eye_mask, 1.0, n) for n in n_ab]
    power = each(lambda n: _dot(n, bd(n)), n_ab)

    def inverse_step():
        nonlocal power, tinv
        both = each(lambda p, t: _dot(jnp.concatenate([p, t], axis=0), bd(p)), power, tinv)
        power = [x[:CHUNK] for x in both]
        tinv = each(lambda t, x: t + x[CHUNK:], tinv, both)

    for c in range(1, WKV_CHUNKS):
        recurrence(c)
        inverse_step()
    for p in range(N_PAIRS):
        state_ref[p] = state[p]

    y = jnp.concatenate(y, axis=0)
    y_hi = y.astype(BF16)
    y_lo = (y - y_hi.astype(F32)).astype(BF16)
    d = y - jnp.dot(jnp.concatenate([y_hi, y_lo], axis=1), gmean, preferred_element_type=F32)
    for _ in range(WKV_CHUNKS - 1, 4):
        inverse_step()
    var = jnp.dot((d * d).astype(BF16), gmean[:PAIR], preferred_element_type=F32)
    tinv = each(lambda t, p: t + _dot(t, bd(p)), tinv, power)
    yn = d * lax.rsqrt(var + GN_EPS)
    gated = []
    for i, (rw, ln) in enumerate(tiles):
        yi = yn[i * CHUNK:(i + 1) * CHUNK] * lng_ref[:, ln] + lnb_ref[:, ln]
        gated.append(((yi + bonus_ref[rw, ln].astype(F32))
                      * sg_ref[rw, ln].astype(F32)).astype(BF16))
    y_rwkv = jnp.concatenate(
        [jnp.concatenate(gated[c * N_PAIRS:(c + 1) * N_PAIRS], axis=1) for c in range(WKV_CHUNKS)],
        axis=0)

    akv = each(lambda a_, v_: _dot(a_, bd(v_)), a_ak, v)

    x_new = (x_ref[...]
             + jnp.dot(y_rwkv, wout_ref[:RWKV_WIDTH, :], preferred_element_type=F32)
             + jnp.dot(yconv_ref[...], wout_ref[RWKV_WIDTH:, :], preferred_element_type=F32))
    if final:
        x_new = (x_new * lax.rsqrt(jnp.mean(x_new * x_new, axis=-1, keepdims=True) + NORM_EPS)
                 * fgain_ref[...])
    x_out[...] = x_new

    pq = each(lambda t, a_, q: _dot(t, jnp.concatenate([bd(a_), bd(q)], axis=1)), tinv, aa, akv)

    def fold_t(m):
        t = bd(m).T
        return t[:CHUNK] + t[CHUNK:]

    ghmn = each(lambda rb, rk, b_, k_, x, v_: _dot(
        jnp.concatenate([jnp.concatenate([rb, rk], axis=1),
                         jnp.concatenate([fold_t(b_), fold_t(k_)], axis=1)], axis=0),
        jnp.concatenate([jnp.concatenate([bd(x[:, :PAIR]), bd(x[:, PAIR:])], axis=1),
                         jnp.concatenate([zeros_pp, bd(v_)], axis=1)], axis=0)),
        a_rb, a_rk, be, ke, pq, v)
    for i in range(WKV_ENTRIES):
        gm_ref[i] = jnp.concatenate(
            [ghmn[i][:CHUNK, :PAIR] + ra[i],
             ghmn[i][CHUNK:, :PAIR] + jnp.where(eye_mask, w_end[i], 0.0)], axis=0).astype(BF16)
        hn_ref[i] = ghmn[i][:, PAIR:]


def _wkv_call(ops, wend, aux, x, lng, lnb, gmean, wout, fgain):
    B, T, _ = x.shape
    n_t = T // WKV_ROWS
    final = fgain is not None
    n_blocks = B * n_t
    split = lambda s: (s // n_t, s % n_t)
    now = lambda s: (*split(jnp.minimum(s, n_blocks - 1)), 0)
    lagged = lambda s: (*split(jnp.maximum(s - 1, 0)), 0)
    blk = lambda width, index_map: pl.BlockSpec((None, WKV_ROWS, width), index_map)
    wend_spec = pl.BlockSpec((None, None, WKV_CHUNKS, RWKV_WIDTH), lambda s: (*now(s), 0))
    const = lambda a: pl.BlockSpec(a.shape, lambda s: (0, 0))
    in_specs = [blk(N_OPS * RWKV_WIDTH, now), wend_spec, blk(N_AUX * RWKV_WIDTH, lagged),
                blk(D_MODEL, lagged), const(lng), const(lnb), const(gmean), const(wout)]
    args = [ops, wend, aux, x, lng, lnb, gmean, wout]
    if final:
        in_specs.append(const(fgain))
        args.append(fgain)
    return pl.pallas_call(
        functools.partial(_wkv_kernel, final, n_t),
        grid=(n_blocks + 1,),
        in_specs=in_specs,
        out_specs=blk(D_MODEL, lagged),
        out_shape=jax.ShapeDtypeStruct((B, T, D_MODEL), F32),
        scratch_shapes=[pltpu.VMEM((N_PAIRS, CHUNK, PAIR), F32),
                        pltpu.VMEM((WKV_ENTRIES, 2 * CHUNK, PAIR), BF16),
                        pltpu.VMEM((WKV_ENTRIES, 2 * CHUNK, PAIR), F32)],
        compiler_params=pltpu.CompilerParams(
            dimension_semantics=("arbitrary",), vmem_limit_bytes=VMEM_LIMIT),
    )(*args)


def kernel(x, norm_gain, w_in, shift_mu, decay_up, decay_bias, aaa_up, aaa_bias, k_k, k_a, r_k,
           ln_gain, ln_bias, v_down, v_up, v_bias, conv_w, w_out, final_gain):
    depth = w_in.shape[0]
    B, T, _ = x.shape
    assert T % WKV_ROWS == 0 and T % PROJ_ROWS == 0
    assert PROJ_ROWS % CHUNK == 0 and HEAD_SIZE == CHUNK
    assert 1 <= WKV_CHUNKS <= 5

    idx = jnp.arange(MXU_DIM)
    hones = (idx[:, None] // HEAD_SIZE == idx[None, :] // HEAD_SIZE).astype(BF16)
    pidx = jnp.arange(PAIR)
    gmean = ((pidx[:, None] // HEAD_SIZE == pidx[None, :] // HEAD_SIZE).astype(F32)
             / HEAD_SIZE).astype(BF16)
    gmean = jnp.concatenate([gmean, gmean], axis=0)
    lora_pad = jnp.zeros((D_LORA, RWKV_WIDTH), F32)

    row = lambda a: a.reshape(1, -1)
    v_first = None
    for l in range(depth):
        dup = jnp.concatenate([decay_up[l], lora_pad], axis=0).astype(BF16)
        aup = jnp.concatenate([lora_pad, aaa_up[l]], axis=0).astype(BF16)
        vres = None
        if l > 0:
            vres = (v_first, v_down[l - 1].astype(BF16), v_up[l - 1].astype(BF16), row(v_bias[l - 1]))
        new_v_first, outs = _proj_call(
            x, row(norm_gain[l]), w_in[l].astype(BF16), row(shift_mu[l]), dup, row(decay_bias[l]),
            aup, row(aaa_bias[l]), row(k_k[l]), row(k_a[l]), row(r_k[l]), conv_w[l], hones, vres)
        if l == 0:
            v_first = new_v_first
        ops, wend, aux = outs
        wend = wend.reshape(B, T // WKV_ROWS, WKV_CHUNKS, RWKV_WIDTH)
        fgain = row(final_gain) if l == depth - 1 else None
        x = _wkv_call(ops, wend, aux, x, row(ln_gain[l]), row(ln_bias[l]), gmean,
                      w_out[l].astype(BF16), fgain)
    return x
```
